```python
import jax, jax.numpy as jnp
from jax import lax
import numpy as np

D_MODEL = 1024
BATCH = 16
SEQ = 256
DEPTH = 4
DEC_BATCH = 4
DEC_SEQ = 1024
PAST_LEN = 512

GRID_W = 64
N_MIXERS = 2
N_ATTN_LAYERS = (DEPTH + N_MIXERS - 1) // N_MIXERS
N_CONV_LAYERS = DEPTH // N_MIXERS
HEAD_DIM = 64
N_HEADS = D_MODEL // HEAD_DIM
N_KV_HEADS = N_HEADS // 4
Q_DIM = N_HEADS * HEAD_DIM
KV_DIM = N_KV_HEADS * HEAD_DIM
QKV_DIM = Q_DIM + 2 * KV_DIM
ROPE_AXIS_DIM = HEAD_DIM // 2
ROPE_THETA = 10000.0
Q_BLOCK = 128
CONV_K = 31
N_EXPERTS = 16
CAPACITY_FACTOR = 2
EXPERT_FF = 2 * D_MODEL
N_MOD = 6
EPS = 1e-6

kernel_name = "hybrid_diffusion_attn_conformer_ecmoe_step"


def rms_norm(x, g):
    xf = x.astype(jnp.float32)
    y = xf * lax.rsqrt(jnp.mean(xf * xf, axis=-1, keepdims=True) + EPS)
    return (y * g.astype(jnp.float32)).astype(x.dtype)


def layer_norm(x, g, b):
    xf = x.astype(jnp.float32)
    mu = jnp.mean(xf, axis=-1, keepdims=True)
    var = jnp.mean(jnp.square(xf - mu), axis=-1, keepdims=True)
    y = (xf - mu) * lax.rsqrt(var + EPS)
    return (y * g.astype(jnp.float32) + b.astype(jnp.float32)).astype(x.dtype)


def adaln_params(cond, w, b):
    m = jax.nn.silu(cond) @ w + b
    return [t[:, None, :] for t in jnp.split(m, N_MOD, axis=-1)]


def modulate(h, shift, scale):
    return h * (1.0 + scale) + shift


def axial_rope_tables(L, dtype):
    rows = L // GRID_W
    row = jnp.repeat(jnp.arange(rows, dtype=jnp.float32), GRID_W)
    col = jnp.tile(jnp.arange(GRID_W, dtype=jnp.float32), rows)
    inv = 1.0 / (ROPE_THETA ** (jnp.arange(0, ROPE_AXIS_DIM, 2, dtype=jnp.float32) / ROPE_AXIS_DIM))
    ang_r = (row[:, None] * inv[None, :])[:, None, :]
    ang_c = (col[:, None] * inv[None, :])[:, None, :]
    return (jnp.cos(ang_r).astype(dtype), jnp.sin(ang_r).astype(dtype),
            jnp.cos(ang_c).astype(dtype), jnp.sin(ang_c).astype(dtype))


def rotate_axis(x, cos, sin):
    x1, x2 = jnp.split(x, 2, axis=-1)
    return jnp.concatenate([x1 * cos - x2 * sin, x2 * cos + x1 * sin], axis=-1)


def apply_axial_rope(x, tables):
    cr, sr, cc, sc = tables
    return jnp.concatenate([rotate_axis(x[..., :ROPE_AXIS_DIM], cr, sr),
                            rotate_axis(x[..., ROPE_AXIS_DIM:], cc, sc)], axis=-1)


def qkv_project(h, w_qkv, q_g, k_g):
    B, L, _ = h.shape
    qkv = h @ w_qkv
    q = qkv[..., :Q_DIM].reshape(B, L, N_HEADS, HEAD_DIM)
    k = qkv[..., Q_DIM:Q_DIM + KV_DIM].reshape(B, L, N_KV_HEADS, HEAD_DIM)
    v = qkv[..., Q_DIM + KV_DIM:].reshape(B, L, N_KV_HEADS, HEAD_DIM)
    return rms_norm(q, q_g), rms_norm(k, k_g), v


def block_attention(q, k, v):
    B, Lq, H, dh = q.shape
    kvh = k.shape[2]
    grp = H // kvh
    nb = Lq // Q_BLOCK
    qb = q.reshape(B, nb, Q_BLOCK, kvh, grp, dh).transpose(1, 0, 2, 3, 4, 5)
    scale = dh ** -0.5

    def one_block(qblk):
        s = jnp.einsum('bqkgd,bskd->bkgqs', qblk, k, preferred_element_type=jnp.float32) * scale
        p = jax.nn.softmax(s, axis=-1).astype(v.dtype)
        return jnp.einsum('bkgqs,bskd->bqkgd', p, v)

    ob = lax.map(one_block, qb)
    return ob.transpose(1, 0, 2, 3, 4, 5).reshape(B, Lq, H * dh)


def depthwise_conv(u, w, b):
    y = lax.conv_general_dilated(u, w[:, None, :], window_strides=(1,),
                                 padding=[(CONV_K // 2, CONV_K // 2)],
                                 dimension_numbers=('NWC', 'WIO', 'NWC'),
                                 feature_group_count=u.shape[-1])
    return y + b


def conformer_conv(h, w1, b1, wdw, bdw, lng, lnb, w2, b2):
    a, g = jnp.split(h @ w1 + b1, 2, axis=-1)
    u = a * jax.nn.sigmoid(g)
    u = depthwise_conv(u, wdw, bdw)
    u = jax.nn.silu(layer_norm(u, lng, lnb))
    return u @ w2 + b2


def expert_choice_moe(h, w_router, w_gate, w_up, w_down):
    B, L, D = h.shape
    cap = (CAPACITY_FACTOR * L) // N_EXPERTS
    logits = jnp.einsum('bld,de->ble', h, w_router, preferred_element_type=jnp.float32)
    aff = jax.nn.softmax(logits, axis=-1)
    gate_vals, idx = lax.top_k(aff.transpose(0, 2, 1), cap)
    xs = jax.vmap(lambda hb, ib: hb[ib])(h, idx)
    a = jnp.einsum('becd,edf->becf', xs, w_gate)
    u = jnp.einsum('becd,edf->becf', xs, w_up)
    y = jnp.einsum('becf,efd->becd', jax.nn.silu(a) * u, w_down)
    y = y * gate_vals[..., None].astype(y.dtype)
    return jax.vmap(lambda yb, ib: jnp.zeros((L, D), yb.dtype).at[ib.reshape(-1)].add(yb.reshape(-1, D)))(y, idx)


def setup_inputs(seed: int = 0) -> dict:
    key = jax.random.key(seed)
    ks = jax.random.split(key, 32)
    nrm = lambda k, shape, s: jax.random.normal(k, shape, jnp.float32) * s
    D = D_MODEL
    return {
        "x_prompt": nrm(ks[0], (BATCH, SEQ, D), 1.0),
        "x_sample": nrm(ks[1], (DEC_BATCH, DEC_SEQ, D), 1.0),
        "cache_k": nrm(ks[2], (DEC_BATCH, N_ATTN_LAYERS, PAST_LEN, N_KV_HEADS, HEAD_DIM), 1.0),
        "cache_v": nrm(ks[3], (DEC_BATCH, N_ATTN_LAYERS, PAST_LEN, N_KV_HEADS, HEAD_DIM), 1.0),
        "c": nrm(ks[4], (DEC_BATCH, D), 1.0),
        "c_ctx": nrm(ks[5], (D,), 1.0),
        "w_mod": nrm(ks[6], (DEPTH, D, N_MOD * D), 0.5 * D ** -0.5),
        "b_mod": nrm(ks[7], (DEPTH, N_MOD * D), 0.02),
        "norm_mix": 1.0 + nrm(ks[8], (DEPTH, D), 0.02),
        "norm_ffn": 1.0 + nrm(ks[9], (DEPTH, D), 0.02),
        "attn_w_qkv": nrm(ks[10], (N_ATTN_LAYERS, D, QKV_DIM), D ** -0.5),
        "attn_q_norm": 1.0 + nrm(ks[11], (N_ATTN_LAYERS, HEAD_DIM), 0.02),
        "attn_k_norm": 1.0 + nrm(ks[12], (N_ATTN_LAYERS, HEAD_DIM), 0.02),
        "attn_w_o": nrm(ks[13], (N_ATTN_LAYERS, Q_DIM, D), Q_DIM ** -0.5),
        "conv_w_pw1": nrm(ks[14], (N_CONV_LAYERS, D, 2 * D), D ** -0.5),
        "conv_b_pw1": nrm(ks[15], (N_CONV_LAYERS, 2 * D), 0.02),
        "conv_w_dw": nrm(ks[16], (N_CONV_LAYERS, CONV_K, D), CONV_K ** -0.5),
        "conv_b_dw": nrm(ks[17], (N_CONV_LAYERS, D), 0.02),
        "conv_ln_g": 1.0 + nrm(ks[18], (N_CONV_LAYERS, D), 0.02),
        "conv_ln_b": nrm(ks[19], (N_CONV_LAYERS, D), 0.02),
        "conv_w_pw2": nrm(ks[20], (N_CONV_LAYERS, D, D), D ** -0.5),
        "conv_b_pw2": nrm(ks[21], (N_CONV_LAYERS, D), 0.02),
        "moe_w_router": nrm(ks[22], (DEPTH, D, N_EXPERTS), D ** -0.5),
        "moe_w_gate": nrm(ks[23], (DEPTH, N_EXPERTS, D, EXPERT_FF), D ** -0.5),
        "moe_w_up": nrm(ks[24], (DEPTH, N_EXPERTS, D, EXPERT_FF), D ** -0.5),
        "moe_w_down": nrm(ks[25], (DEPTH, N_EXPERTS, EXPERT_FF, D), EXPERT_FF ** -0.5),
    }


def reference(x_prompt, x_sample, cache_k, cache_v, c, c_ctx, w_mod, b_mod, norm_mix, norm_ffn,
              attn_w_qkv, attn_q_norm, attn_k_norm, attn_w_o,
              conv_w_pw1, conv_b_pw1, conv_w_dw, conv_b_dw, conv_ln_g, conv_ln_b, conv_w_pw2, conv_b_pw2,
              moe_w_router, moe_w_gate, moe_w_up, moe_w_down):
    xp, xs = x_prompt, x_sample
    rope = axial_rope_tables(xs.shape[1], xs.dtype)
    new_k, new_v = [], []
    for i in range(DEPTH):
        j = i // N_MIXERS
        p_sh_a, p_sc_a, p_g_a, p_sh_f, p_sc_f, p_g_f = adaln_params(c_ctx[None, :], w_mod[i], b_mod[i])
        s_sh_a, s_sc_a, s_g_a, s_sh_f, s_sc_f, s_g_f = adaln_params(c, w_mod[i], b_mod[i])
        hp = modulate(rms_norm(xp, norm_mix[i]), p_sh_a, p_sc_a)
        hs = modulate(rms_norm(xs, norm_mix[i]), s_sh_a, s_sc_a)
        if i % N_MIXERS == 0:
            qp, kp, vp = qkv_project(hp, attn_w_qkv[j], attn_q_norm[j], attn_k_norm[j])
            out_p = block_attention(qp, kp, vp) @ attn_w_o[j]
            new_k.append(kp)
            new_v.append(vp)
            qs, ks_, vs = qkv_project(hs, attn_w_qkv[j], attn_q_norm[j], attn_k_norm[j])
            qs = apply_axial_rope(qs, rope)
            ks_ = apply_axial_rope(ks_, rope)
            k_all = jnp.concatenate([cache_k[:, j], ks_], axis=1)
            v_all = jnp.concatenate([cache_v[:, j], vs], axis=1)
            out_s = block_attention(qs, k_all, v_all) @ attn_w_o[j]
        else:
            conv_args = (conv_w_pw1[j], conv_b_pw1[j], conv_w_dw[j], conv_b_dw[j],
                         conv_ln_g[j], conv_ln_b[j], conv_w_pw2[j], conv_b_pw2[j])
            out_p = conformer_conv(hp, *conv_args)
            out_s = conformer_conv(hs, *conv_args)
        xp = xp + p_g_a * out_p
        xs = xs + s_g_a * out_s
        moe_args = (moe_w_router[i], moe_w_gate[i], moe_w_up[i], moe_w_down[i])
        hp = modulate(rms_norm(xp, norm_ffn[i]), p_sh_f, p_sc_f)
        hs = modulate(rms_norm(xs, norm_ffn[i]), s_sh_f, s_sc_f)
        xp = xp + p_g_f * expert_choice_moe(hp, *moe_args)
        xs = xs + s_g_f * expert_choice_moe(hs, *moe_args)
    new_cache_k = jnp.stack(new_k, axis=1)
    new_cache_v = jnp.stack(new_v, axis=1)
    return (xp, xs, new_cache_k, new_cache_v)
```

```python
import functools

import numpy as np

import jax
import jax.numpy as jnp
from jax import lax
from jax.experimental import pallas as pl
from jax.experimental.pallas import tpu as pltpu

F32 = jnp.float32
BF16 = jnp.bfloat16
I32 = jnp.int32

D_MODEL = 1024
BATCH = 16
SEQ = 256
DEPTH = 4
DEC_BATCH = 4
DEC_SEQ = 1024
PAST_LEN = 512
GRID_W = 64
N_MIXERS = 2
HEAD_DIM = 64
N_HEADS = 16
N_KV_HEADS = 4
N_GROUPS = N_HEADS // N_KV_HEADS
KV_DIM = N_KV_HEADS * HEAD_DIM
QKV_DIM = D_MODEL + 2 * KV_DIM
ROPE_AXIS_DIM = HEAD_DIM // 2
ROPE_THETA = 10000.0
CONV_K = 31
N_EXPERTS = 16
EXPERT_FF = 2 * D_MODEL
N_MOD = 6
EPS = 1e-6

N_PROMPT_TOK = BATCH * SEQ
N_TOK = N_PROMPT_TOK + DEC_BATCH * DEC_SEQ
SUPER = 1024
N_SUPER = N_TOK // SUPER
N_PROMPT_SUPER = N_PROMPT_TOK // SUPER
SEQ_PER_SUPER = SUPER // SEQ
SLOTS = SUPER // 8
SEQ_SLOTS = SEQ // 8
LANES = 128
TILE_M = 512
N_TILES = N_TOK // TILE_M
N_PROMPT_TILES = N_PROMPT_TOK // TILE_M
ATTN_TQ = 256
FF_CHUNK = 1024
N_FF_CHUNKS = EXPERT_FF // FF_CHUNK
CONV_ROWS = 64
CONV_HALO = 16
GATHER_EXPERTS = 4
VMEM_LIMIT = 56 * 1024 * 1024


def _dot(a, b):
    return jnp.dot(a, b, preferred_element_type=F32)


def _dot_nt(a, b):
    return lax.dot_general(a, b, (((1,), (1,)), ((), ())), preferred_element_type=F32)


def _sigmoid(x):
    return 1.0 / (1.0 + jnp.exp(-x))


def _norm_mod(x, g, shift, scale):
    y = x * lax.rsqrt(jnp.mean(x * x, axis=-1, keepdims=True) + EPS) * g
    return y * (1.0 + scale) + shift


def _params(semantics):
    return pltpu.CompilerParams(dimension_semantics=semantics, vmem_limit_bytes=VMEM_LIMIT)


def _mod_spec(layer, which, super_of):
    return pl.BlockSpec((1, 1, 3, 1, 1, D_MODEL), lambda *idx: (layer, which, 0, super_of(*idx), 0, 0))


def _tile_super(i):
    return i // (SUPER // TILE_M)


def _x_specs(n_x, rows, n_prompt_blocks):
    if n_x == 1:
        return [pl.BlockSpec((rows, D_MODEL), lambda i: (i, 0))]
    return [pl.BlockSpec((rows, D_MODEL), lambda i: (jnp.minimum(i, n_prompt_blocks - 1), 0)),
            pl.BlockSpec((rows, D_MODEL), lambda i: (jnp.maximum(i - n_prompt_blocks, 0), 0))]


def _read_x(x_refs, i, n_prompt_blocks):
    if len(x_refs) == 1:
        return x_refs[0][...]
    return jnp.where(i < n_prompt_blocks, x_refs[0][...], x_refs[1][...])


def _adaln_kernel(cond_ref, w_ref, b_ref, o_ref):
    c = cond_ref[...]
    s = (c * _sigmoid(c)).astype(BF16)
    o_ref[0, 0] = _dot(s, w_ref[0].astype(BF16)) + b_ref[0, 0]


def _adaln(cond8, w_mod, b_mod):
    b4 = b_mod.reshape(DEPTH, N_MOD, 1, D_MODEL)
    out = pl.pallas_call(
        _adaln_kernel,
        grid=(DEPTH, N_MOD),
        in_specs=[
            pl.BlockSpec((N_SUPER, D_MODEL), lambda l, n: (0, 0)),
            pl.BlockSpec((1, D_MODEL, D_MODEL), lambda l, n: (l, 0, n)),
            pl.BlockSpec((1, 1, 1, D_MODEL), lambda l, n: (l, n, 0, 0)),
        ],
        out_specs=pl.BlockSpec((1, 1, N_SUPER, D_MODEL), lambda l, n: (l, n, 0, 0)),
        out_shape=jax.ShapeDtypeStruct((DEPTH, N_MOD, N_SUPER, D_MODEL), F32),
        compiler_params=_params(("arbitrary", "arbitrary")),
        name="adaln",
    )(cond8, w_mod, b4)
    return out.reshape(DEPTH, 2, 3, N_SUPER, 1, D_MODEL)


def _qkv_kernel(*refs, n_x):
    x_refs = refs[:n_x]
    (m_ref, g_ref, w_ref, qg_ref, kg_ref, cos_ref, sin_ref, bd_ref,
     q_ref, k_ref, v_ref, kc_ref, vc_ref, wbf_ref) = refs[n_x:]
    i = pl.program_id(0)

    @pl.when(i == 0)
    def _():
        wbf_ref[...] = w_ref[0].astype(BF16)

    x = _read_x(x_refs, i, N_PROMPT_TILES)
    h = _norm_mod(x, g_ref[...], m_ref[0, 0, 0, 0], m_ref[0, 0, 1, 0]).astype(BF16)
    qkv = _dot(h, wbf_ref[...])
    bd = bd_ref[...]
    cos = cos_ref[...]
    sin = sin_ref[...]
    lane = lax.broadcasted_iota(I32, (TILE_M, LANES), 1)
    first_half = (lane & (ROPE_AXIS_DIM // 2)) == 0
    low_head = lane < HEAD_DIM
    latent = i >= N_PROMPT_TILES

    def head_norm(t, gain):
        sq = t * t
        hi = sq.astype(BF16)
        lo = (sq - hi.astype(F32)).astype(BF16)
        ssum = _dot(hi, bd) + _dot(lo, bd)
        return t * lax.rsqrt(ssum * (1.0 / HEAD_DIM) + EPS) * gain

    def rope(y):
        partner = jnp.where(first_half, pltpu.roll(y, LANES - ROPE_AXIS_DIM // 2, 1),
                            pltpu.roll(y, ROPE_AXIS_DIM // 2, 1))
        return jnp.where(latent, y * cos + partner * sin, y)

    qg = qg_ref[...]
    kg = kg_ref[...]
    q_tiles = []
    for j in range(D_MODEL // LANES):
        y = head_norm(qkv[:, j * LANES:(j + 1) * LANES], qg)
        q_tiles.append(rope(y) * (HEAD_DIM ** -0.5))
    for t in range(D_MODEL // LANES):
        grp = t // 2
        src = (4 * (2 * (t % 2)) + grp) // 2
        lo_part = q_tiles[src] if grp % 2 == 0 else pltpu.roll(q_tiles[src], HEAD_DIM, 1)
        hi_part = q_tiles[src + 2] if grp % 2 == 1 else pltpu.roll(q_tiles[src + 2], HEAD_DIM, 1)
        q_ref[:, t * LANES:(t + 1) * LANES] = jnp.where(low_head, lo_part, hi_part).astype(BF16)
    for j in range(KV_DIM // LANES):
        y = head_norm(qkv[:, D_MODEL + j * LANES:D_MODEL + (j + 1) * LANES], kg)
        k_ref[:, j * LANES:(j + 1) * LANES] = rope(y).astype(BF16)

        @pl.when(i < N_PROMPT_TILES)
        def _():
            kc_ref[:, j * LANES:(j + 1) * LANES] = y
    v = qkv[:, D_MODEL + KV_DIM:]
    v_ref[...] = v.astype(BF16)

    @pl.when(i < N_PROMPT_TILES)
    def _():
        vc_ref[...] = v


def _qkv(xs, mod, layer, g, w_qkv, j, qg2, kg2, cos_t, sin_t, bd):
    tile = lambda i: (i, 0)
    const = lambda i: (0, 0)
    cache_tile = lambda i: (jnp.minimum(i, N_PROMPT_TILES - 1), 0)
    pos_tile = lambda i: (i % (DEC_SEQ // TILE_M), 0)
    return pl.pallas_call(
        functools.partial(_qkv_kernel, n_x=len(xs)),
        grid=(N_TILES,),
        in_specs=_x_specs(len(xs), TILE_M, N_PROMPT_TILES) + [
            _mod_spec(layer, 0, _tile_super),
            pl.BlockSpec((1, D_MODEL), const),
            pl.BlockSpec((1, D_MODEL, QKV_DIM), lambda i: (j, 0, 0)),
            pl.BlockSpec((1, LANES), const),
            pl.BlockSpec((1, LANES), const),
            pl.BlockSpec((TILE_M, LANES), pos_tile),
            pl.BlockSpec((TILE_M, LANES), pos_tile),
            pl.BlockSpec((LANES, LANES), const),
        ],
        out_specs=[
            pl.BlockSpec((TILE_M, D_MODEL), tile),
            pl.BlockSpec((TILE_M, KV_DIM), tile),
            pl.BlockSpec((TILE_M, KV_DIM), tile),
            pl.BlockSpec((TILE_M, KV_DIM), cache_tile),
            pl.BlockSpec((TILE_M, KV_DIM), cache_tile),
        ],
        out_shape=[
            jax.ShapeDtypeStruct((N_TOK, D_MODEL), BF16),
            jax.ShapeDtypeStruct((N_TOK, KV_DIM), BF16),
            jax.ShapeDtypeStruct((N_TOK, KV_DIM), BF16),
            jax.ShapeDtypeStruct((N_PROMPT_TOK, KV_DIM), F32),
            jax.ShapeDtypeStruct((N_PROMPT_TOK, KV_DIM), F32),
        ],
        scratch_shapes=[pltpu.VMEM((D_MODEL, QKV_DIM), BF16)],
        compiler_params=_params(("arbitrary",)),
        name="qkv_proj",
    )(*xs, mod, g, w_qkv, qg2, kg2, cos_t, sin_t, bd)


def _attn_heads(q_ref, k_all, v_all, o_ref):
    lane = lax.broadcasted_iota(I32, (ATTN_TQ, KV_DIM), 1)
    for g in range(N_GROUPS):
        qs = q_ref[:, g * KV_DIM:(g + 1) * KV_DIM]
        acc = jnp.zeros((ATTN_TQ, KV_DIM), F32)
        for kv in range(N_KV_HEADS):
            mask = (lane >= kv * HEAD_DIM) & (lane < (kv + 1) * HEAD_DIM)
            s = _dot_nt(jnp.where(mask, qs, jnp.zeros_like(qs)), k_all)
            p = jnp.exp(s - jnp.max(s, axis=-1, keepdims=True))
            inv = 1.0 / jnp.sum(p, axis=-1, keepdims=True)
            o = _dot(p.astype(BF16), v_all)
            acc = acc + jnp.where(mask, o * inv, 0.0)
        o_ref[:, g * KV_DIM:(g + 1) * KV_DIM] = acc.astype(BF16)


def _attn_prompt_kernel(q_ref, k_ref, v_ref, o_ref):
    _attn_heads(q_ref, k_ref[...], v_ref[...], o_ref)


def _attn_sample_kernel(q_ref, k_ref, v_ref, ck_ref, cv_ref, o_ref, kall_ref, vall_ref):
    @pl.when(pl.program_id(1) == 0)
    def _():
        kall_ref[:PAST_LEN] = ck_ref[0, 0].astype(BF16)
        kall_ref[PAST_LEN:] = k_ref[...]
        vall_ref[:PAST_LEN] = cv_ref[0, 0].astype(BF16)
        vall_ref[PAST_LEN:] = v_ref[...]

    _attn_heads(q_ref, kall_ref[...], vall_ref[...], o_ref)


def _attention(q, k, v, cache_k4, cache_v4, j):
    o_p = pl.pallas_call(
        _attn_prompt_kernel,
        grid=(BATCH,),
        in_specs=[
            pl.BlockSpec((SEQ, D_MODEL), lambda b: (b, 0)),
            pl.BlockSpec((SEQ, KV_DIM), lambda b: (b, 0)),
            pl.BlockSpec((SEQ, KV_DIM), lambda b: (b, 0)),
        ],
        out_specs=pl.BlockSpec((SEQ, D_MODEL), lambda b: (b, 0)),
        out_shape=jax.ShapeDtypeStruct((N_PROMPT_TOK, D_MODEL), BF16),
        compiler_params=_params(("arbitrary",)),
        name="attn_prompt",
    )(q, k, v)
    q_blocks = DEC_SEQ // ATTN_TQ
    first_q = N_PROMPT_TOK // ATTN_TQ
    first_k = N_PROMPT_TOK // DEC_SEQ
    o_s = pl.pallas_call(
        _attn_sample_kernel,
        grid=(DEC_BATCH, q_blocks),
        in_specs=[
            pl.BlockSpec((ATTN_TQ, D_MODEL), lambda b, t: (first_q + b * q_blocks + t, 0)),
            pl.BlockSpec((DEC_SEQ, KV_DIM), lambda b, t: (first_k + b, 0)),
            pl.BlockSpec((DEC_SEQ, KV_DIM), lambda b, t: (first_k + b, 0)),
            pl.BlockSpec((1, 1, PAST_LEN, KV_DIM), lambda b, t: (b, j, 0, 0)),
            pl.BlockSpec((1, 1, PAST_LEN, KV_DIM), lambda b, t: (b, j, 0, 0)),
        ],
        out_specs=pl.BlockSpec((ATTN_TQ, D_MODEL), lambda b, t: (b * q_blocks + t, 0)),
        out_shape=jax.ShapeDtypeStruct((DEC_BATCH * DEC_SEQ, D_MODEL), BF16),
        scratch_shapes=[pltpu.VMEM((PAST_LEN + DEC_SEQ, KV_DIM), BF16),
                        pltpu.VMEM((PAST_LEN + DEC_SEQ, KV_DIM), BF16)],
        compiler_params=_params(("arbitrary", "arbitrary")),
        name="attn_sample",
    )(q, k, v, cache_k4, cache_v4)
    return o_p, o_s


def _oproj_kernel(*refs, n_x):
    x_refs = refs[:n_x]
    op_ref, os_ref = refs[n_x:n_x + 2]
    w_refs = refs[n_x + 2:n_x + 2 + N_GROUPS]
    m_ref, out_ref, wbf_ref = refs[n_x + 2 + N_GROUPS:]
    i = pl.program_id(0)

    @pl.when(i == 0)
    def _():
        for g in range(N_GROUPS):
            wbf_ref[g * KV_DIM:(g + 1) * KV_DIM, :] = w_refs[g][0, :, 0].reshape(KV_DIM, D_MODEL).astype(BF16)

    o = jnp.where(i < N_PROMPT_TILES, op_ref[...], os_ref[...])
    out_ref[...] = _read_x(x_refs, i, N_PROMPT_TILES) + m_ref[0, 0, 2, 0] * _dot(o, wbf_ref[...])


def _oproj(xs, o_p, o_s, w_o5, j, mod, layer):
    w_specs = [pl.BlockSpec((1, N_KV_HEADS, 1, HEAD_DIM, D_MODEL), lambda i, g=g: (j, 0, g, 0, 0))
               for g in range(N_GROUPS)]
    return pl.pallas_call(
        functools.partial(_oproj_kernel, n_x=len(xs)),
        grid=(N_TILES,),
        in_specs=_x_specs(len(xs), TILE_M, N_PROMPT_TILES) + [
            pl.BlockSpec((TILE_M, D_MODEL), lambda i: (jnp.minimum(i, N_PROMPT_TILES - 1), 0)),
            pl.BlockSpec((TILE_M, D_MODEL), lambda i: (jnp.maximum(i - N_PROMPT_TILES, 0), 0)),
        ] + w_specs + [_mod_spec(layer, 0, _tile_super)],
        out_specs=pl.BlockSpec((TILE_M, D_MODEL), lambda i: (i, 0)),
        out_shape=jax.ShapeDtypeStruct((N_TOK, D_MODEL), F32),
        scratch_shapes=[pltpu.VMEM((D_MODEL, D_MODEL), BF16)],
        compiler_params=_params(("arbitrary",)),
        name="attn_out_proj",
    )(*xs, o_p, o_s, *([w_o5] * N_GROUPS), mod)


def _conv_glu_kernel(x_ref, m_ref, g_ref, w_ref, b_ref, u_ref, wbf_ref):
    @pl.when(pl.program_id(0) == 0)
    def _():
        wbf_ref[...] = w_ref[0].astype(BF16)

    h = _norm_mod(x_ref[...], g_ref[...], m_ref[0, 0, 0, 0], m_ref[0, 0, 1, 0]).astype(BF16)
    z = _dot(h, wbf_ref[...]) + b_ref[0]
    u_ref[...] = z[:, :D_MODEL] * _sigmoid(z[:, D_MODEL:])


def _conv_glu(x, mod, layer, g, w1, b1, j):
    return pl.pallas_call(
        _conv_glu_kernel,
        grid=(N_TILES,),
        in_specs=[
            pl.BlockSpec((TILE_M, D_MODEL), lambda i: (i, 0)),
            _mod_spec(layer, 0, _tile_super),
            pl.BlockSpec((1, D_MODEL), lambda i: (0, 0)),
            pl.BlockSpec((1, D_MODEL, 2 * D_MODEL), lambda i: (j, 0, 0)),
            pl.BlockSpec((1, 1, 2 * D_MODEL), lambda i: (j, 0, 0)),
        ],
        out_specs=pl.BlockSpec((TILE_M, D_MODEL), lambda i: (i, 0)),
        out_shape=jax.ShapeDtypeStruct((N_TOK, D_MODEL), F32),
        scratch_shapes=[pltpu.VMEM((D_MODEL, 2 * D_MODEL), BF16)],
        compiler_params=_params(("arbitrary",)),
        name="conv_glu",
    )(x, mod, g, w1, b1)


PAD_ROWS = CONV_HALO + SEQ_PER_SUPER * (SEQ + CONV_HALO)


def _conv_tail_kernel(x_ref, u_ref, wdw_ref, bdw_ref, lng_ref, lnb_ref, w2_ref, b2_ref, m_ref,
                      out_ref, pad_ref, cv_ref, wbf_ref):
    ss = pl.program_id(0)

    @pl.when(ss == 0)
    def _():
        wbf_ref[...] = w2_ref[0].astype(BF16)

    pad_ref[...] = jnp.zeros_like(pad_ref)

    @pl.when(ss < N_PROMPT_SUPER)
    def _():
        for s in range(SEQ_PER_SUPER):
            for j in range(D_MODEL // LANES):
                pad_ref[j, pl.ds(CONV_HALO + s * (SEQ + CONV_HALO), SEQ), :] = (
                    u_ref[pl.ds(s * SEQ, SEQ), j * LANES:(j + 1) * LANES])

    @pl.when(ss >= N_PROMPT_SUPER)
    def _():
        for j in range(D_MODEL // LANES):
            pad_ref[j, pl.ds(CONV_HALO, SUPER), :] = u_ref[:, j * LANES:(j + 1) * LANES]

    seq_gap = jnp.where(ss < N_PROMPT_SUPER, CONV_HALO, 0)

    def block(c, carry):
        r0 = pl.multiple_of(c * CONV_ROWS, CONV_ROWS)
        base = r0 + (c // (SEQ // CONV_ROWS)) * seq_gap + (CONV_HALO - CONV_K // 2)
        for j in range(D_MODEL // LANES):
            cols = slice(j * LANES, (j + 1) * LANES)
            acc = jnp.zeros((CONV_ROWS, LANES), F32)
            for kk in range(CONV_K):
                acc = acc + wdw_ref[0, pl.ds(kk, 1), cols] * pad_ref[j, pl.ds(base + kk, CONV_ROWS), :]
            cv_ref[pl.ds(r0, CONV_ROWS), cols] = acc + bdw_ref[0, :, cols]
        return carry

    lax.fori_loop(0, SUPER // CONV_ROWS, block, 0)

    cv = cv_ref[...]
    mu = jnp.mean(cv, axis=-1, keepdims=True)
    cen = cv - mu
    var = jnp.mean(cen * cen, axis=-1, keepdims=True)
    y = cen * lax.rsqrt(var + EPS) * lng_ref[0] + lnb_ref[0]
    act = (y * _sigmoid(y)).astype(BF16)
    out_ref[...] = x_ref[...] + m_ref[0, 0, 2, 0] * (_dot(act, wbf_ref[...]) + b2_ref[0])


def _conv_tail(x, u, wdw, bdw, lng, lnb, w2, b2, mod, layer, j):
    vec = pl.BlockSpec((1, 1, D_MODEL), lambda s: (j, 0, 0))
    return pl.pallas_call(
        _conv_tail_kernel,
        grid=(N_SUPER,),
        in_specs=[
            pl.BlockSpec((SUPER, D_MODEL), lambda s: (s, 0)),
            pl.BlockSpec((SUPER, D_MODEL), lambda s: (s, 0)),
            pl.BlockSpec((1, CONV_K, D_MODEL), lambda s: (j, 0, 0)),
            vec, vec, vec,
            pl.BlockSpec((1, D_MODEL, D_MODEL), lambda s: (j, 0, 0)),
            vec,
            _mod_spec(layer, 0, lambda s: s),
        ],
        out_specs=pl.BlockSpec((SUPER, D_MODEL), lambda s: (s, 0)),
        out_shape=jax.ShapeDtypeStruct((N_TOK, D_MODEL), F32),
        scratch_shapes=[pltpu.VMEM((D_MODEL // LANES, PAD_ROWS, LANES), F32),
                        pltpu.VMEM((SUPER, D_MODEL), F32),
                        pltpu.VMEM((D_MODEL, D_MODEL), BF16)],
        compiler_params=_params(("arbitrary",)),
        name="conv_tail",
    )(x, u, wdw, bdw, lng, lnb, w2, b2, mod)


def _route(aff, cap, tri):
    rows = aff.shape[0]
    capf = float(cap)

    def count_ge(bits):
        return jnp.sum((aff >= lax.bitcast_convert_type(bits, F32)).astype(F32), axis=1, keepdims=True)

    thr = jnp.zeros((rows, 1), I32)
    top = thr | (1 << 30)
    thr = jnp.where(count_ge(top) >= capf, top, thr)
    for shift in range(27, -1, -3):
        digit = jnp.zeros((rows, 1), I32)
        for d in range(1, 8):
            digit = digit + (count_ge(thr | (d << shift)) >= capf).astype(I32)
        thr = thr | (digit << shift)
    thr_f = lax.bitcast_convert_type(thr, F32)
    above = aff > thr_f
    tied = aff == thr_f
    need = capf - jnp.sum(above.astype(F32), axis=1, keepdims=True)
    tied_rank = _dot(tied.astype(BF16), tri)
    sel = above | (tied & (tied_rank <= need))
    return jnp.where(sel, _dot(sel.astype(BF16), tri) - 1.0, -1.0)


def _pad_rows_t(a, fill):
    pad = jnp.full((LANES - N_EXPERTS, a.shape[1]), fill, F32)
    return jnp.concatenate([a, pad], axis=0).T


def _moe_select_kernel(x_ref, m_ref, g_ref, wrt_ref, xs_ref, slot_ref, gate_ref, h_ref, tri_ref, p_ref):
    ss = pl.program_id(0)

    @pl.when(ss == 0)
    def _():
        r = lax.broadcasted_iota(I32, (SUPER, SUPER), 0)
        c = lax.broadcasted_iota(I32, (SUPER, SUPER), 1)
        tri_ref[...] = (r <= c).astype(BF16)

    h_ref[...] = _norm_mod(x_ref[...], g_ref[...], m_ref[0, 0, 0, 0], m_ref[0, 0, 1, 0]).astype(BF16)

    logit = _dot_nt(wrt_ref[0].astype(BF16), h_ref[...])
    ex = jnp.exp(logit - jnp.max(logit, axis=0, keepdims=True))
    aff = ex / jnp.sum(ex, axis=0, keepdims=True)

    @pl.when(ss < N_PROMPT_SUPER)
    def _():
        aff_r = jnp.concatenate([aff[:, b * SEQ:(b + 1) * SEQ] for b in range(SEQ_PER_SUPER)], axis=0)
        pos = _route(aff_r, SEQ_SLOTS, tri_ref[:SEQ, :SEQ])
        sub = lax.broadcasted_iota(I32, (SEQ_SLOTS, SEQ), 0).astype(F32)
        for b in range(SEQ_PER_SUPER):
            for e in range(N_EXPERTS):
                r = b * N_EXPERTS + e
                p_ref[e * SEQ_SLOTS:(e + 1) * SEQ_SLOTS, :SEQ] = (sub == pos[r:r + 1, :]).astype(BF16)
            rows = _dot(p_ref[:N_EXPERTS * SEQ_SLOTS, :SEQ], h_ref[b * SEQ:(b + 1) * SEQ, :])
            xs_ref[:, b * SEQ_SLOTS:(b + 1) * SEQ_SLOTS, :] = (
                rows.astype(BF16).reshape(N_EXPERTS, SEQ_SLOTS, D_MODEL))
            slot_ref[b * SEQ:(b + 1) * SEQ, :] = _pad_rows_t(pos[b * N_EXPERTS:(b + 1) * N_EXPERTS], -1.0)
            gate_ref[b * SEQ:(b + 1) * SEQ, :] = _pad_rows_t(aff[:, b * SEQ:(b + 1) * SEQ], 0.0)

    @pl.when(ss >= N_PROMPT_SUPER)
    def _():
        pos = _route(aff, SLOTS, tri_ref[...])
        sub = lax.broadcasted_iota(I32, (SLOTS, SUPER), 0).astype(F32)
        for e in range(N_EXPERTS):
            p_ref[e * SLOTS:(e + 1) * SLOTS, :] = (sub == pos[e:e + 1, :]).astype(BF16)
        for e0 in range(0, N_EXPERTS, GATHER_EXPERTS):
            rows = _dot(p_ref[e0 * SLOTS:(e0 + GATHER_EXPERTS) * SLOTS, :], h_ref[...])
            xs_ref[e0:e0 + GATHER_EXPERTS] = rows.astype(BF16).reshape(GATHER_EXPERTS, SLOTS, D_MODEL)
        slot_ref[...] = _pad_rows_t(pos, -1.0)
        gate_ref[...] = _pad_rows_t(aff, 0.0)


def _moe_select(x, mod, layer, g, wr_t):
    return pl.pallas_call(
        _moe_select_kernel,
        grid=(N_SUPER,),
        in_specs=[
            pl.BlockSpec((SUPER, D_MODEL), lambda s: (s, 0)),
            _mod_spec(layer, 1, lambda s: s),
            pl.BlockSpec((1, D_MODEL), lambda s: (0, 0)),
            pl.BlockSpec((1, N_EXPERTS, D_MODEL), lambda s: (layer, 0, 0)),
        ],
        out_specs=[
            pl.BlockSpec((N_EXPERTS, SLOTS, D_MODEL), lambda s: (0, s, 0)),
            pl.BlockSpec((SUPER, LANES), lambda s: (s, 0)),
            pl.BlockSpec((SUPER, LANES), lambda s: (s, 0)),
        ],
        out_shape=[
            jax.ShapeDtypeStruct((N_EXPERTS, N_SUPER * SLOTS, D_MODEL), BF16),
            jax.ShapeDtypeStruct((N_TOK, LANES), F32),
            jax.ShapeDtypeStruct((N_TOK, LANES), F32),
        ],
        scratch_shapes=[pltpu.VMEM((SUPER, D_MODEL), BF16),
                        pltpu.VMEM((SUPER, SUPER), BF16),
                        pltpu.VMEM((N_EXPERTS * SLOTS, SUPER), BF16)],
        compiler_params=_params(("arbitrary",)),
        name="moe_select",
    )(x, mod, g, wr_t)


def _moe_ffn_kernel(x_ref, wg_ref, wu_ref, wd_ref, y_ref, acc_ref):
    f = pl.program_id(1)
    x = x_ref[0]
    a = _dot(x, wg_ref[0, 0].astype(BF16))
    u = _dot(x, wu_ref[0, 0].astype(BF16))
    hid = (a * _sigmoid(a) * u).astype(BF16)
    part = _dot(hid, wd_ref[0, 0].astype(BF16))

    @pl.when(f == 0)
    def _():
        acc_ref[...] = part

    @pl.when(f == N_FF_CHUNKS - 1)
    def _():
        y_ref[0] = (acc_ref[...] + part).astype(BF16)


def _moe_ffn(xs, w_gate, w_up, w_down, layer):
    assert N_FF_CHUNKS == 2
    rows = N_SUPER * SLOTS
    return pl.pallas_call(
        _moe_ffn_kernel,
        grid=(N_EXPERTS, N_FF_CHUNKS),
        in_specs=[
            pl.BlockSpec((1, rows, D_MODEL), lambda e, f: (e, 0, 0)),
            pl.BlockSpec((1, 1, D_MODEL, FF_CHUNK), lambda e, f: (layer, e, 0, f)),
            pl.BlockSpec((1, 1, D_MODEL, FF_CHUNK), lambda e, f: (layer, e, 0, f)),
            pl.BlockSpec((1, 1, FF_CHUNK, D_MODEL), lambda e, f: (layer, e, f, 0)),
        ],
        out_specs=pl.BlockSpec((1, rows, D_MODEL), lambda e, f: (e, 0, 0)),
        out_shape=jax.ShapeDtypeStruct((N_EXPERTS, rows, D_MODEL), BF16),
        scratch_shapes=[pltpu.VMEM((rows, D_MODEL), F32)],
        compiler_params=_params(("arbitrary", "arbitrary")),
        name="moe_ffn",
    )(xs, w_gate, w_up, w_down)


def _moe_combine_kernel(x_ref, y_ref, slot_ref, gate_ref, spread_ref, m_ref, *refs, final):
    out_refs, s_ref = refs[:-1], refs[-1]
    ss = pl.program_id(0)
    gate_f = m_ref[0, 0, 2, 0]

    @pl.when(ss < N_PROMPT_SUPER)
    def _():
        lane_slot = (lax.broadcasted_iota(I32, (SEQ, LANES), 1) % SEQ_SLOTS).astype(F32)
        for b in range(SEQ_PER_SUPER):
            rows = pl.ds(b * SEQ, SEQ)
            slot_b = slot_ref[rows, :].astype(BF16)
            gate_b = gate_ref[rows, :].astype(BF16)
            for t in range(N_EXPERTS * SEQ_SLOTS // LANES):
                hit = _dot(slot_b, spread_ref[t]) == lane_slot
                s_ref[rows, t * LANES:(t + 1) * LANES] = jnp.where(hit, _dot(gate_b, spread_ref[t]), 0.0).astype(BF16)
            y_b = y_ref[:, b * SEQ_SLOTS:(b + 1) * SEQ_SLOTS, :].reshape(N_EXPERTS * SEQ_SLOTS, D_MODEL)
            res = x_ref[rows, :] + gate_f * _dot(s_ref[rows, :N_EXPERTS * SEQ_SLOTS], y_b)
            out_refs[0][rows, :] = res

    @pl.when(ss >= N_PROMPT_SUPER)
    def _():
        lane = lax.broadcasted_iota(I32, (SUPER, SLOTS), 1).astype(F32)
        for e in range(N_EXPERTS):
            hit = slot_ref[:, e:e + 1] == lane
            s_ref[:, e * SLOTS:(e + 1) * SLOTS] = jnp.where(hit, gate_ref[:, e:e + 1], 0.0).astype(BF16)
        y = y_ref[...].reshape(N_EXPERTS * SLOTS, D_MODEL)
        out_refs[-1][...] = x_ref[...] + gate_f * _dot(s_ref[...], y)


def _moe_combine(x, ys, slot_t, gate_t, spread, mod, layer, final):
    if final:
        out_specs = [pl.BlockSpec((SUPER, D_MODEL), lambda s: (jnp.minimum(s, N_PROMPT_SUPER - 1), 0)),
                     pl.BlockSpec((SUPER, D_MODEL), lambda s: (jnp.maximum(s - N_PROMPT_SUPER, 0), 0))]
        out_shape = [jax.ShapeDtypeStruct((N_PROMPT_TOK, D_MODEL), F32),
                     jax.ShapeDtypeStruct((N_TOK - N_PROMPT_TOK, D_MODEL), F32)]
    else:
        out_specs = [pl.BlockSpec((SUPER, D_MODEL), lambda s: (s, 0))]
        out_shape = [jax.ShapeDtypeStruct((N_TOK, D_MODEL), F32)]
    return pl.pallas_call(
        functools.partial(_moe_combine_kernel, final=final),
        grid=(N_SUPER,),
        in_specs=[
            pl.BlockSpec((SUPER, D_MODEL), lambda s: (s, 0)),
            pl.BlockSpec((N_EXPERTS, SLOTS, D_MODEL), lambda s: (0, s, 0)),
            pl.BlockSpec((SUPER, LANES), lambda s: (s, 0)),
            pl.BlockSpec((SUPER, LANES), lambda s: (s, 0)),
            pl.BlockSpec((N_EXPERTS * SEQ_SLOTS // LANES, LANES, LANES), lambda s: (0, 0, 0)),
            _mod_spec(layer, 1, lambda s: s),
        ],
        out_specs=out_specs,
        out_shape=out_shape,
        scratch_shapes=[pltpu.VMEM((SUPER, N_EXPERTS * SLOTS), BF16)],
        compiler_params=_params(("arbitrary",)),
        name="moe_combine",
    )(x, ys, slot_t, gate_t, spread, mod)


def _rope_tables():
    t = np.arange(DEC_SEQ)
    row = (t // GRID_W).astype(np.float32)
    col = (t % GRID_W).astype(np.float32)
    inv = (1.0 / (ROPE_THETA ** (np.arange(0, ROPE_AXIS_DIM, 2, dtype=np.float32) / ROPE_AXIS_DIM))).astype(np.float32)
    d = np.arange(LANES) % HEAD_DIM
    use_col = (d // ROPE_AXIS_DIM) == 1
    second_half = ((d % ROPE_AXIS_DIM) // (ROPE_AXIS_DIM // 2)) == 1
    ang = (np.where(use_col[None, :], col[:, None], row[:, None]) * inv[d % (ROPE_AXIS_DIM // 2)][None, :]).astype(np.float32)
    cos = np.cos(ang).astype(np.float32)
    sin = (np.where(second_half[None, :], 1.0, -1.0) * np.sin(ang)).astype(np.float32)
    return jnp.asarray(cos), jnp.asarray(sin)


def _head_block_ones():
    lane = np.arange(LANES)
    return jnp.asarray((lane[:, None] // HEAD_DIM) == (lane[None, :] // HEAD_DIM), dtype=BF16)


def _spread_tables():
    tiles = N_EXPERTS * SEQ_SLOTS // LANES
    e = np.arange(LANES)[None, :, None]
    lane = np.arange(LANES)[None, None, :]
    tile = np.arange(tiles)[:, None, None]
    return jnp.asarray(e == tile * (LANES // SEQ_SLOTS) + lane // SEQ_SLOTS, dtype=BF16)


def kernel(x_prompt, x_sample, cache_k, cache_v, c, c_ctx, w_mod, b_mod, norm_mix, norm_ffn, attn_w_qkv, attn_q_norm, attn_k_norm, attn_w_o, conv_w_pw1, conv_b_pw1, conv_w_dw, conv_b_dw, conv_ln_g, conv_ln_b, conv_w_pw2, conv_b_pw2, moe_w_router, moe_w_gate, moe_w_up, moe_w_down):
    xs = [x_prompt.reshape(N_PROMPT_TOK, D_MODEL), x_sample.reshape(N_TOK - N_PROMPT_TOK, D_MODEL)]
    cond8 = jnp.concatenate([jnp.broadcast_to(c_ctx[None, :], (N_PROMPT_SUPER, D_MODEL)), c], axis=0)
    mod = _adaln(cond8, w_mod, b_mod)

    cos_t, sin_t = _rope_tables()
    bd = _head_block_ones()
    spread = _spread_tables()
    wr_t = jnp.swapaxes(moe_w_router, 1, 2)
    cache_k4 = cache_k.reshape(DEC_BATCH, -1, PAST_LEN, KV_DIM)
    cache_v4 = cache_v.reshape(DEC_BATCH, -1, PAST_LEN, KV_DIM)
    w_o5 = attn_w_o.reshape(-1, N_KV_HEADS, N_GROUPS, HEAD_DIM, D_MODEL)

    new_k, new_v = [], []
    for i in range(DEPTH):
        j = i // N_MIXERS
        g_mix = norm_mix[i][None, :]
        if i % N_MIXERS == 0:
            qg2 = jnp.tile(attn_q_norm[j], LANES // HEAD_DIM)[None, :]
            kg2 = jnp.tile(attn_k_norm[j], LANES // HEAD_DIM)[None, :]
            q, k, v, kc, vc = _qkv(xs, mod, i, g_mix, attn_w_qkv, j, qg2, kg2, cos_t, sin_t, bd)
            new_k.append(kc.reshape(BATCH, SEQ, N_KV_HEADS, HEAD_DIM))
            new_v.append(vc.reshape(BATCH, SEQ, N_KV_HEADS, HEAD_DIM))
            o_p, o_s = _attention(q, k, v, cache_k4, cache_v4, j)
            x = _oproj(xs, o_p, o_s, w_o5, j, mod, i)
        else:
            u = _conv_glu(xs[0], mod, i, g_mix, conv_w_pw1, conv_b_pw1[:, None, :], j)
            x = _conv_tail(xs[0], u, conv_w_dw, conv_b_dw[:, None, :], conv_ln_g[:, None, :], conv_ln_b[:, None, :],
                           conv_w_pw2, conv_b_pw2[:, None, :], mod, i, j)
        xg, slot_t, gate_t = _moe_select(x, mod, i, norm_ffn[i][None, :], wr_t)
        ys = _moe_ffn(xg, moe_w_gate, moe_w_up, moe_w_down, i)
        xs = _moe_combine(x, ys, slot_t, gate_t, spread, mod, i, final=(i == DEPTH - 1))

    y_prompt = xs[0].reshape(BATCH, SEQ, D_MODEL)
    y_sample = xs[1].reshape(DEC_BATCH, DEC_SEQ, D_MODEL)
    return (y_prompt, y_sample, jnp.stack(new_k, axis=1), jnp.stack(new_v, axis=1))
```

```python
import functools

import numpy as np

import jax
import jax.numpy as jnp
from jax import lax
from jax.experimental import pallas as pl
from jax.experimental.pallas import tpu as pltpu

F32 = jnp.float32
BF16 = jnp.bfloat16
I32 = jnp.int32

D_MODEL = 1024
BATCH = 16
SEQ = 256
DEPTH = 4
DEC_BATCH = 4
DEC_SEQ = 1024
PAST_LEN = 512
GRID_W = 64
N_MIXERS = 2
HEAD_DIM = 64
N_HEADS = 16
N_KV_HEADS = 4
N_GROUPS = N_HEADS // N_KV_HEADS
KV_DIM = N_KV_HEADS * HEAD_DIM
QKV_DIM = D_MODEL + 2 * KV_DIM
ROPE_AXIS_DIM = HEAD_DIM // 2
ROPE_THETA = 10000.0
CONV_K = 31
N_EXPERTS = 16
EXPERT_FF = 2 * D_MODEL
N_MOD = 6
EPS = 1e-6

N_PROMPT_TOK = BATCH * SEQ
N_TOK = N_PROMPT_TOK + DEC_BATCH * DEC_SEQ
SUPER = 1024
N_SUPER = N_TOK // SUPER
N_PROMPT_SUPER = N_PROMPT_TOK // SUPER
SEQ_PER_SUPER = SUPER // SEQ
SLOTS = SUPER // 8
SEQ_SLOTS = SEQ // 8
LANES = 128
TILE_M = 512
N_TILES = N_TOK // TILE_M
N_PROMPT_TILES = N_PROMPT_TOK // TILE_M
QKV_SUB = 256
ATTN_TQ = 256
FF_CHUNK = 1024
N_FF_CHUNKS = EXPERT_FF // FF_CHUNK
CONV_ROWS = 64
CONV_HALO = 16
GATHER_EXPERTS = 4
SCATTER_EXPERTS = 2
VMEM_LIMIT = 56 * 1024 * 1024


def _dot(a, b):
    return jnp.dot(a, b, preferred_element_type=F32)


def _dot_nt(a, b):
    return lax.dot_general(a, b, (((1,), (1,)), ((), ())), preferred_element_type=F32)


def _sigmoid(x):
    return 1.0 / (1.0 + jnp.exp(-x))


def _norm_mod(x, g, shift, scale):
    y = x * lax.rsqrt(jnp.mean(x * x, axis=-1, keepdims=True) + EPS) * g
    return y * (1.0 + scale) + shift


def _params(semantics):
    return pltpu.CompilerParams(dimension_semantics=semantics, vmem_limit_bytes=VMEM_LIMIT)


def _mod_spec(layer, which, super_of):
    return pl.BlockSpec((1, 1, 3, 1, 1, D_MODEL), lambda *idx: (layer, which, 0, super_of(*idx), 0, 0))


def _tile_super(i):
    return i // (SUPER // TILE_M)


def _x_specs(n_x, rows, n_prompt_blocks):
    if n_x == 1:
        return [pl.BlockSpec((rows, D_MODEL), lambda i: (i, 0))]
    return [pl.BlockSpec((rows, D_MODEL), lambda i: (jnp.minimum(i, n_prompt_blocks - 1), 0)),
            pl.BlockSpec((rows, D_MODEL), lambda i: (jnp.maximum(i - n_prompt_blocks, 0), 0))]


def _read_x(x_refs, i, n_prompt_blocks):
    if len(x_refs) == 1:
        return x_refs[0][...]
    return jnp.where(i < n_prompt_blocks, x_refs[0][...], x_refs[1][...])


def _adaln_kernel(cond_ref, w_ref, b_ref, o_ref):
    c = cond_ref[...]
    s = (c * _sigmoid(c)).astype(BF16)
    o_ref[0, 0] = _dot(s, w_ref[0].astype(BF16)) + b_ref[0, 0]


def _adaln(cond8, w_mod, b_mod):
    b4 = b_mod.reshape(DEPTH, N_MOD, 1, D_MODEL)
    out = pl.pallas_call(
        _adaln_kernel,
        grid=(DEPTH, N_MOD),
        in_specs=[
            pl.BlockSpec((N_SUPER, D_MODEL), lambda l, n: (0, 0)),
            pl.BlockSpec((1, D_MODEL, D_MODEL), lambda l, n: (l, 0, n)),
            pl.BlockSpec((1, 1, 1, D_MODEL), lambda l, n: (l, n, 0, 0)),
        ],
        out_specs=pl.BlockSpec((1, 1, N_SUPER, D_MODEL), lambda l, n: (l, n, 0, 0)),
        out_shape=jax.ShapeDtypeStruct((DEPTH, N_MOD, N_SUPER, D_MODEL), F32),
        compiler_params=_params(("arbitrary", "arbitrary")),
        name="adaln",
    )(cond8, w_mod, b4)
    return out.reshape(DEPTH, 2, 3, N_SUPER, 1, D_MODEL)


def _qkv_kernel(*refs, n_x):
    x_refs = refs[:n_x]
    (m_ref, g_ref, w_ref, qg_ref, kg_ref, cos_ref, sin_ref, bd_ref,
     q_ref, k_ref, v_ref, kc_ref, vc_ref, wbf_ref) = refs[n_x:]
    i = pl.program_id(0)

    @pl.when(i == 0)
    def _():
        wbf_ref[...] = w_ref[0].astype(BF16)

    bd = bd_ref[...]
    lane = lax.broadcasted_iota(I32, (QKV_SUB, LANES), 1)
    first_half = (lane & (ROPE_AXIS_DIM // 2)) == 0
    low_head = lane < HEAD_DIM
    latent = i >= N_PROMPT_SUPER
    qg = qg_ref[...] * (HEAD_DIM ** -0.5)
    kg = kg_ref[...]
    g_mix = g_ref[...]
    shift = m_ref[0, 0, 0, 0]
    scale = m_ref[0, 0, 1, 0]

    def head_norm(t, gain):
        sq = t * t
        hi = sq.astype(BF16)
        lo = (sq - hi.astype(F32)).astype(BF16)
        ssum = _dot(hi, bd) + _dot(lo, bd)
        return t * lax.rsqrt(ssum * (1.0 / HEAD_DIM) + EPS) * gain

    for sb in range(SUPER // QKV_SUB):
        rows = pl.ds(sb * QKV_SUB, QKV_SUB)
        x = x_refs[0][rows, :] if n_x == 1 else jnp.where(latent, x_refs[1][rows, :], x_refs[0][rows, :])
        h = _norm_mod(x, g_mix, shift, scale).astype(BF16)
        qkv = _dot(h, wbf_ref[...])
        cos = cos_ref[rows, :]
        sin = sin_ref[rows, :]

        def rope(y):
            partner = jnp.where(first_half, pltpu.roll(y, LANES - ROPE_AXIS_DIM // 2, 1),
                                pltpu.roll(y, ROPE_AXIS_DIM // 2, 1))
            return jnp.where(latent, y * cos + partner * sin, y)

        q_tiles = [rope(head_norm(qkv[:, j * LANES:(j + 1) * LANES], qg)) for j in range(D_MODEL // LANES)]
        for t in range(D_MODEL // LANES):
            grp = t // 2
            src = (4 * (2 * (t % 2)) + grp) // 2
            lo_part = q_tiles[src] if grp % 2 == 0 else pltpu.roll(q_tiles[src], HEAD_DIM, 1)
            hi_part = q_tiles[src + 2] if grp % 2 == 1 else pltpu.roll(q_tiles[src + 2], HEAD_DIM, 1)
            q_ref[rows, t * LANES:(t + 1) * LANES] = jnp.where(low_head, lo_part, hi_part).astype(BF16)
        for j in range(KV_DIM // LANES):
            y = head_norm(qkv[:, D_MODEL + j * LANES:D_MODEL + (j + 1) * LANES], kg)
            k_ref[rows, j * LANES:(j + 1) * LANES] = rope(y).astype(BF16)
            kc_ref[rows, j * LANES:(j + 1) * LANES] = y
        v = qkv[:, D_MODEL + KV_DIM:]
        v_ref[rows, :] = v.astype(BF16)
        vc_ref[rows, :] = v


def _qkv(xs, mod, layer, g, w_qkv, j, qg2, kg2, cos_t, sin_t, bd):
    tile = lambda i: (i, 0)
    const = lambda i: (0, 0)
    return pl.pallas_call(
        functools.partial(_qkv_kernel, n_x=len(xs)),
        grid=(N_SUPER,),
        in_specs=_x_specs(len(xs), SUPER, N_PROMPT_SUPER) + [
            _mod_spec(layer, 0, lambda i: i),
            pl.BlockSpec((1, D_MODEL), const),
            pl.BlockSpec((1, D_MODEL, QKV_DIM), lambda i: (j, 0, 0)),
            pl.BlockSpec((1, LANES), const),
            pl.BlockSpec((1, LANES), const),
            pl.BlockSpec((DEC_SEQ, LANES), const),
            pl.BlockSpec((DEC_SEQ, LANES), const),
            pl.BlockSpec((LANES, LANES), const),
        ],
        out_specs=[
            pl.BlockSpec((SUPER, D_MODEL), tile),
            pl.BlockSpec((SUPER, KV_DIM), tile),
            pl.BlockSpec((SUPER, KV_DIM), tile),
            pl.BlockSpec((SUPER, KV_DIM), tile),
            pl.BlockSpec((SUPER, KV_DIM), tile),
        ],
        out_shape=[
            jax.ShapeDtypeStruct((N_TOK, D_MODEL), BF16),
            jax.ShapeDtypeStruct((N_TOK, KV_DIM), BF16),
            jax.ShapeDtypeStruct((N_TOK, KV_DIM), BF16),
            jax.ShapeDtypeStruct((N_TOK, KV_DIM), F32),
            jax.ShapeDtypeStruct((N_TOK, KV_DIM), F32),
        ],
        scratch_shapes=[pltpu.VMEM((D_MODEL, QKV_DIM), BF16)],
        compiler_params=_params(("arbitrary",)),
        name="qkv_proj",
    )(*xs, mod, g, w_qkv, qg2, kg2, cos_t, sin_t, bd)


def _attn_heads(q_ref, k_all, v_all, o_ref):
    lane = lax.broadcasted_iota(I32, (ATTN_TQ, KV_DIM), 1)
    for g in range(N_GROUPS):
        qs = q_ref[:, g * KV_DIM:(g + 1) * KV_DIM]
        acc = jnp.zeros((ATTN_TQ, KV_DIM), F32)
        for kv in range(N_KV_HEADS):
            mask = (lane >= kv * HEAD_DIM) & (lane < (kv + 1) * HEAD_DIM)
            s = _dot_nt(jnp.where(mask, qs, jnp.zeros_like(qs)), k_all)
            p = jnp.exp(s - jnp.max(s, axis=-1, keepdims=True))
            inv = 1.0 / jnp.sum(p, axis=-1, keepdims=True)
            o = _dot(p.astype(BF16), v_all)
            acc = acc + jnp.where(mask, o * inv, 0.0)
        o_ref[:, g * KV_DIM:(g + 1) * KV_DIM] = acc.astype(BF16)


def _attn_prompt_kernel(q_ref, k_ref, v_ref, o_ref):
    _attn_heads(q_ref, k_ref[...], v_ref[...], o_ref)


def _attn_sample_kernel(q_ref, k_ref, v_ref, ck_ref, cv_ref, o_ref, kall_ref, vall_ref):
    @pl.when(pl.program_id(1) == 0)
    def _():
        kall_ref[:PAST_LEN] = ck_ref[0, 0].astype(BF16)
        kall_ref[PAST_LEN:] = k_ref[...]
        vall_ref[:PAST_LEN] = cv_ref[0, 0].astype(BF16)
        vall_ref[PAST_LEN:] = v_ref[...]

    _attn_heads(q_ref, kall_ref[...], vall_ref[...], o_ref)


def _attention(q, k, v, cache_k4, cache_v4, j):
    o_p = pl.pallas_call(
        _attn_prompt_kernel,
        grid=(BATCH,),
        in_specs=[
            pl.BlockSpec((SEQ, D_MODEL), lambda b: (b, 0)),
            pl.BlockSpec((SEQ, KV_DIM), lambda b: (b, 0)),
            pl.BlockSpec((SEQ, KV_DIM), lambda b: (b, 0)),
        ],
        out_specs=pl.BlockSpec((SEQ, D_MODEL), lambda b: (b, 0)),
        out_shape=jax.ShapeDtypeStruct((N_PROMPT_TOK, D_MODEL), BF16),
        compiler_params=_params(("arbitrary",)),
        name="attn_prompt",
    )(q, k, v)
    q_blocks = DEC_SEQ // ATTN_TQ
    first_q = N_PROMPT_TOK // ATTN_TQ
    first_k = N_PROMPT_TOK // DEC_SEQ
    o_s = pl.pallas_call(
        _attn_sample_kernel,
        grid=(DEC_BATCH, q_blocks),
        in_specs=[
            pl.BlockSpec((ATTN_TQ, D_MODEL), lambda b, t: (first_q + b * q_blocks + t, 0)),
            pl.BlockSpec((DEC_SEQ, KV_DIM), lambda b, t: (first_k + b, 0)),
            pl.BlockSpec((DEC_SEQ, KV_DIM), lambda b, t: (first_k + b, 0)),
            pl.BlockSpec((1, 1, PAST_LEN, KV_DIM), lambda b, t: (b, j, 0, 0)),
            pl.BlockSpec((1, 1, PAST_LEN, KV_DIM), lambda b, t: (b, j, 0, 0)),
        ],
        out_specs=pl.BlockSpec((ATTN_TQ, D_MODEL), lambda b, t: (b * q_blocks + t, 0)),
        out_shape=jax.ShapeDtypeStruct((DEC_BATCH * DEC_SEQ, D_MODEL), BF16),
        scratch_shapes=[pltpu.VMEM((PAST_LEN + DEC_SEQ, KV_DIM), BF16),
                        pltpu.VMEM((PAST_LEN + DEC_SEQ, KV_DIM), BF16)],
        compiler_params=_params(("arbitrary", "arbitrary")),
        name="attn_sample",
    )(q, k, v, cache_k4, cache_v4)
    return o_p, o_s


def _oproj_kernel(*refs, n_x):
    x_refs = refs[:n_x]
    op_ref, os_ref = refs[n_x:n_x + 2]
    w_refs = refs[n_x + 2:n_x + 2 + N_GROUPS]
    m_ref, out_ref, wbf_ref = refs[n_x + 2 + N_GROUPS:]
    i = pl.program_id(0)

    @pl.when(i == 0)
    def _():
        for g in range(N_GROUPS):
            wbf_ref[g * KV_DIM:(g + 1) * KV_DIM, :] = w_refs[g][0, :, 0].reshape(KV_DIM, D_MODEL).astype(BF16)

    o = jnp.where(i < N_PROMPT_TILES, op_ref[...], os_ref[...])
    out_ref[...] = _read_x(x_refs, i, N_PROMPT_TILES) + m_ref[0, 0, 2, 0] * _dot(o, wbf_ref[...])


def _oproj(xs, o_p, o_s, w_o5, j, mod, layer):
    w_specs = [pl.BlockSpec((1, N_KV_HEADS, 1, HEAD_DIM, D_MODEL), lambda i, g=g: (j, 0, g, 0, 0))
               for g in range(N_GROUPS)]
    return pl.pallas_call(
        functools.partial(_oproj_kernel, n_x=len(xs)),
        grid=(N_TILES,),
        in_specs=_x_specs(len(xs), TILE_M, N_PROMPT_TILES) + [
            pl.BlockSpec((TILE_M, D_MODEL), lambda i: (jnp.minimum(i, N_PROMPT_TILES - 1), 0)),
            pl.BlockSpec((TILE_M, D_MODEL), lambda i: (jnp.maximum(i - N_PROMPT_TILES, 0), 0)),
        ] + w_specs + [_mod_spec(layer, 0, _tile_super)],
        out_specs=pl.BlockSpec((TILE_M, D_MODEL), lambda i: (i, 0)),
        out_shape=jax.ShapeDtypeStruct((N_TOK, D_MODEL), F32),
        scratch_shapes=[pltpu.VMEM((D_MODEL, D_MODEL), BF16)],
        compiler_params=_params(("arbitrary",)),
        name="attn_out_proj",
    )(*xs, o_p, o_s, *([w_o5] * N_GROUPS), mod)


def _conv_glu_kernel(x_ref, m_ref, g_ref, w_ref, b_ref, u_ref, wbf_ref):
    @pl.when(pl.program_id(0) == 0)
    def _():
        wbf_ref[...] = w_ref[0].astype(BF16)

    h = _norm_mod(x_ref[...], g_ref[...], m_ref[0, 0, 0, 0], m_ref[0, 0, 1, 0]).astype(BF16)
    z = _dot(h, wbf_ref[...]) + b_ref[0]
    u_ref[...] = z[:, :D_MODEL] * _sigmoid(z[:, D_MODEL:])


def _conv_glu(x, mod, layer, g, w1, b1, j):
    return pl.pallas_call(
        _conv_glu_kernel,
        grid=(N_TILES,),
        in_specs=[
            pl.BlockSpec((TILE_M, D_MODEL), lambda i: (i, 0)),
            _mod_spec(layer, 0, _tile_super),
            pl.BlockSpec((1, D_MODEL), lambda i: (0, 0)),
            pl.BlockSpec((1, D_MODEL, 2 * D_MODEL), lambda i: (j, 0, 0)),
            pl.BlockSpec((1, 1, 2 * D_MODEL), lambda i: (j, 0, 0)),
        ],
        out_specs=pl.BlockSpec((TILE_M, D_MODEL), lambda i: (i, 0)),
        out_shape=jax.ShapeDtypeStruct((N_TOK, D_MODEL), F32),
        scratch_shapes=[pltpu.VMEM((D_MODEL, 2 * D_MODEL), BF16)],
        compiler_params=_params(("arbitrary",)),
        name="conv_glu",
    )(x, mod, g, w1, b1)


PAD_ROWS = CONV_HALO + SEQ_PER_SUPER * (SEQ + CONV_HALO)


def _conv_tail_kernel(x_ref, u_ref, wdw_ref, bdw_ref, lng_ref, lnb_ref, w2_ref, b2_ref, m_ref,
                      out_ref, pad_ref, cv_ref, wbf_ref):
    ss = pl.program_id(0)

    @pl.when(ss == 0)
    def _():
        wbf_ref[...] = w2_ref[0].astype(BF16)

    pad_ref[...] = jnp.zeros_like(pad_ref)

    @pl.when(ss < N_PROMPT_SUPER)
    def _():
        for s in range(SEQ_PER_SUPER):
            for j in range(D_MODEL // LANES):
                pad_ref[j, pl.ds(CONV_HALO + s * (SEQ + CONV_HALO), SEQ), :] = (
                    u_ref[pl.ds(s * SEQ, SEQ), j * LANES:(j + 1) * LANES])

    @pl.when(ss >= N_PROMPT_SUPER)
    def _():
        for j in range(D_MODEL // LANES):
            pad_ref[j, pl.ds(CONV_HALO, SUPER), :] = u_ref[:, j * LANES:(j + 1) * LANES]

    seq_gap = jnp.where(ss < N_PROMPT_SUPER, CONV_HALO, 0)

    def block(c, carry):
        r0 = pl.multiple_of(c * CONV_ROWS, CONV_ROWS)
        base = r0 + (c // (SEQ // CONV_ROWS)) * seq_gap + (CONV_HALO - CONV_K // 2)
        for j in range(D_MODEL // LANES):
            cols = slice(j * LANES, (j + 1) * LANES)
            acc = jnp.zeros((CONV_ROWS, LANES), F32)
            for kk in range(CONV_K):
                acc = acc + wdw_ref[0, pl.ds(kk, 1), cols] * pad_ref[j, pl.ds(base + kk, CONV_ROWS), :]
            cv_ref[pl.ds(r0, CONV_ROWS), cols] = acc + bdw_ref[0, :, cols]
        return carry

    lax.fori_loop(0, SUPER // CONV_ROWS, block, 0)

    cv = cv_ref[...]
    mu = jnp.mean(cv, axis=-1, keepdims=True)
    cen = cv - mu
    var = jnp.mean(cen * cen, axis=-1, keepdims=True)
    y = cen * lax.rsqrt(var + EPS) * lng_ref[0] + lnb_ref[0]
    act = (y * _sigmoid(y)).astype(BF16)
    out_ref[...] = x_ref[...] + m_ref[0, 0, 2, 0] * (_dot(act, wbf_ref[...]) + b2_ref[0])


def _conv_tail(x, u, wdw, bdw, lng, lnb, w2, b2, mod, layer, j):
    vec = pl.BlockSpec((1, 1, D_MODEL), lambda s: (j, 0, 0))
    return pl.pallas_call(
        _conv_tail_kernel,
        grid=(N_SUPER,),
        in_specs=[
            pl.BlockSpec((SUPER, D_MODEL), lambda s: (s, 0)),
            pl.BlockSpec((SUPER, D_MODEL), lambda s: (s, 0)),
            pl.BlockSpec((1, CONV_K, D_MODEL), lambda s: (j, 0, 0)),
            vec, vec, vec,
            pl.BlockSpec((1, D_MODEL, D_MODEL), lambda s: (j, 0, 0)),
            vec,
            _mod_spec(layer, 0, lambda s: s),
        ],
        out_specs=pl.BlockSpec((SUPER, D_MODEL), lambda s: (s, 0)),
        out_shape=jax.ShapeDtypeStruct((N_TOK, D_MODEL), F32),
        scratch_shapes=[pltpu.VMEM((D_MODEL // LANES, PAD_ROWS, LANES), F32),
                        pltpu.VMEM((SUPER, D_MODEL), F32),
                        pltpu.VMEM((D_MODEL, D_MODEL), BF16)],
        compiler_params=_params(("arbitrary",)),
        name="conv_tail",
    )(x, u, wdw, bdw, lng, lnb, w2, b2, mod)


def _route(aff, cap, tri):
    rows = aff.shape[0]
    capf = float(cap)

    def count_ge(bits):
        return jnp.sum((aff >= lax.bitcast_convert_type(bits, F32)).astype(F32), axis=1, keepdims=True)

    thr = jnp.zeros((rows, 1), I32)
    top = thr | (1 << 30)
    thr = jnp.where(count_ge(top) >= capf, top, thr)
    for shift in range(27, -1, -3):
        digit = jnp.zeros((rows, 1), I32)
        for d in range(1, 8):
            digit = digit + (count_ge(thr | (d << shift)) >= capf).astype(I32)
        thr = thr | (digit << shift)
    thr_f = lax.bitcast_convert_type(thr, F32)
    above = aff > thr_f
    tied = aff == thr_f
    need = capf - jnp.sum(above.astype(F32), axis=1, keepdims=True)
    tied_rank = _dot(tied.astype(BF16), tri)
    sel = above | (tied & (tied_rank <= need))
    return jnp.where(sel, _dot(sel.astype(BF16), tri) - 1.0, -1.0)


def _pad_rows_t(a, fill):
    pad = jnp.full((LANES - N_EXPERTS, a.shape[1]), fill, F32)
    return jnp.concatenate([a, pad], axis=0).T


def _moe_select_kernel(x_ref, m_ref, g_ref, wrt_ref, xs_ref, slot_ref, gate_ref, h_ref, tri_ref, p_ref):
    ss = pl.program_id(0)

    @pl.when(ss == 0)
    def _():
        r = lax.broadcasted_iota(I32, (SUPER, SUPER), 0)
        c = lax.broadcasted_iota(I32, (SUPER, SUPER), 1)
        tri_ref[...] = (r <= c).astype(BF16)

    h_ref[...] = _norm_mod(x_ref[...], g_ref[...], m_ref[0, 0, 0, 0], m_ref[0, 0, 1, 0]).astype(BF16)

    logit = _dot_nt(wrt_ref[0].astype(BF16), h_ref[...])
    ex = jnp.exp(logit - jnp.max(logit, axis=0, keepdims=True))
    aff = ex / jnp.sum(ex, axis=0, keepdims=True)

    @pl.when(ss < N_PROMPT_SUPER)
    def _():
        aff_r = jnp.concatenate([aff[:, b * SEQ:(b + 1) * SEQ] for b in range(SEQ_PER_SUPER)], axis=0)
        pos = _route(aff_r, SEQ_SLOTS, tri_ref[:SEQ, :SEQ])
        sub = lax.broadcasted_iota(I32, (SEQ_SLOTS, SEQ), 0).astype(F32)
        for b in range(SEQ_PER_SUPER):
            for e in range(N_EXPERTS):
                r = b * N_EXPERTS + e
                p_ref[e * SEQ_SLOTS:(e + 1) * SEQ_SLOTS, :] = (sub == pos[r:r + 1, :]).astype(BF16)
            rows = _dot(p_ref[...], h_ref[b * SEQ:(b + 1) * SEQ, :])
            xs_ref[:, b * SEQ_SLOTS:(b + 1) * SEQ_SLOTS, :] = (
                rows.astype(BF16).reshape(N_EXPERTS, SEQ_SLOTS, D_MODEL))
            slot_ref[b * SEQ:(b + 1) * SEQ, :] = _pad_rows_t(pos[b * N_EXPERTS:(b + 1) * N_EXPERTS], -1.0)
            gate_ref[b * SEQ:(b + 1) * SEQ, :] = _pad_rows_t(aff[:, b * SEQ:(b + 1) * SEQ], 0.0)

    @pl.when(ss >= N_PROMPT_SUPER)
    def _():
        pos = _route(aff, SLOTS, tri_ref[...])
        sub = lax.broadcasted_iota(I32, (SLOTS, SUPER), 0).astype(F32)
        for e0 in range(0, N_EXPERTS, GATHER_EXPERTS):
            onehot = jnp.concatenate([(sub == pos[e:e + 1, :]).astype(BF16)
                                      for e in range(e0, e0 + GATHER_EXPERTS)], axis=0)
            rows = _dot(onehot, h_ref[...])
            xs_ref[e0:e0 + GATHER_EXPERTS] = rows.astype(BF16).reshape(GATHER_EXPERTS, SLOTS, D_MODEL)
        slot_ref[...] = _pad_rows_t(pos, -1.0)
        gate_ref[...] = _pad_rows_t(aff, 0.0)


def _moe_select(x, mod, layer, g, wr_t):
    return pl.pallas_call(
        _moe_select_kernel,
        grid=(N_SUPER,),
        in_specs=[
            pl.BlockSpec((SUPER, D_MODEL), lambda s: (s, 0)),
            _mod_spec(layer, 1, lambda s: s),
            pl.BlockSpec((1, D_MODEL), lambda s: (0, 0)),
            pl.BlockSpec((1, N_EXPERTS, D_MODEL), lambda s: (layer, 0, 0)),
        ],
        out_specs=[
            pl.BlockSpec((N_EXPERTS, SLOTS, D_MODEL), lambda s: (0, s, 0)),
            pl.BlockSpec((SUPER, LANES), lambda s: (s, 0)),
            pl.BlockSpec((SUPER, LANES), lambda s: (s, 0)),
        ],
        out_shape=[
            jax.ShapeDtypeStruct((N_EXPERTS, N_SUPER * SLOTS, D_MODEL), BF16),
            jax.ShapeDtypeStruct((N_TOK, LANES), F32),
            jax.ShapeDtypeStruct((N_TOK, LANES), F32),
        ],
        scratch_shapes=[pltpu.VMEM((SUPER, D_MODEL), BF16),
                        pltpu.VMEM((SUPER, SUPER), BF16),
                        pltpu.VMEM((N_EXPERTS * SEQ_SLOTS, SEQ), BF16)],
        compiler_params=_params(("arbitrary",)),
        name="moe_select",
    )(x, mod, g, wr_t)


def _moe_ffn_kernel(x_ref, wg_ref, wu_ref, wd_ref, y_ref, acc_ref):
    f = pl.program_id(1)
    x = x_ref[0]
    a = _dot(x, wg_ref[0, 0].astype(BF16))
    u = _dot(x, wu_ref[0, 0].astype(BF16))
    hid = (a * _sigmoid(a) * u).astype(BF16)
    acc_ref[f] = _dot(hid, wd_ref[0, 0].astype(BF16))

    @pl.when(f == N_FF_CHUNKS - 1)
    def _():
        y_ref[0] = (acc_ref[0] + acc_ref[1]).astype(BF16)


def _moe_ffn(xs, w_gate, w_up, w_down, layer):
    assert N_FF_CHUNKS == 2
    rows = N_SUPER * SLOTS
    return pl.pallas_call(
        _moe_ffn_kernel,
        grid=(N_EXPERTS, N_FF_CHUNKS),
        in_specs=[
            pl.BlockSpec((1, rows, D_MODEL), lambda e, f: (e, 0, 0)),
            pl.BlockSpec((1, 1, D_MODEL, FF_CHUNK), lambda e, f: (layer, e, 0, f)),
            pl.BlockSpec((1, 1, D_MODEL, FF_CHUNK), lambda e, f: (layer, e, 0, f)),
            pl.BlockSpec((1, 1, FF_CHUNK, D_MODEL), lambda e, f: (layer, e, f, 0)),
        ],
        out_specs=pl.BlockSpec((1, rows, D_MODEL), lambda e, f: (e, 0, 0)),
        out_shape=jax.ShapeDtypeStruct((N_EXPERTS, rows, D_MODEL), BF16),
        scratch_shapes=[pltpu.VMEM((N_FF_CHUNKS, rows, D_MODEL), F32)],
        compiler_params=_params(("arbitrary", "arbitrary")),
        name="moe_ffn",
    )(xs, w_gate, w_up, w_down)


def _moe_combine_kernel(x_ref, y_ref, slot_ref, gate_ref, spread_ref, m_ref, *refs, final):
    out_refs, s_ref = refs[:-1], refs[-1]
    ss = pl.program_id(0)
    gate_f = m_ref[0, 0, 2, 0]

    @pl.when(ss < N_PROMPT_SUPER)
    def _():
        lane_slot = (lax.broadcasted_iota(I32, (SEQ, LANES), 1) % SEQ_SLOTS).astype(F32)
        for b in range(SEQ_PER_SUPER):
            rows = pl.ds(b * SEQ, SEQ)
            slot_b = slot_ref[rows, :].astype(BF16)
            gate_b = gate_ref[rows, :].astype(BF16)
            for t in range(N_EXPERTS * SEQ_SLOTS // LANES):
                hit = _dot(slot_b, spread_ref[t]) == lane_slot
                s_ref[rows, t * LANES:(t + 1) * LANES] = jnp.where(hit, _dot(gate_b, spread_ref[t]), 0.0).astype(BF16)
            y_b = y_ref[:, b * SEQ_SLOTS:(b + 1) * SEQ_SLOTS, :].reshape(N_EXPERTS * SEQ_SLOTS, D_MODEL)
            res = x_ref[rows, :] + gate_f * _dot(s_ref[rows, :N_EXPERTS * SEQ_SLOTS], y_b)
            out_refs[0][rows, :] = res

    @pl.when(ss >= N_PROMPT_SUPER)
    def _():
        lane = lax.broadcasted_iota(I32, (SUPER, SLOTS), 1).astype(F32)
        moe = None
        for e0 in range(0, N_EXPERTS, SCATTER_EXPERTS):
            s_cols = jnp.concatenate(
                [jnp.where(slot_ref[:, e:e + 1] == lane, gate_ref[:, e:e + 1], 0.0).astype(BF16)
                 for e in range(e0, e0 + SCATTER_EXPERTS)], axis=1)
            part = _dot(s_cols, y_ref[e0:e0 + SCATTER_EXPERTS].reshape(SCATTER_EXPERTS * SLOTS, D_MODEL))
            moe = part if moe is None else moe + part
        out_refs[-1][...] = x_ref[...] + gate_f * moe


def _moe_combine(x, ys, slot_t, gate_t, spread, mod, layer, final):
    if final:
        out_specs = [pl.BlockSpec((SUPER, D_MODEL), lambda s: (jnp.minimum(s, N_PROMPT_SUPER - 1), 0)),
                     pl.BlockSpec((SUPER, D_MODEL), lambda s: (jnp.maximum(s - N_PROMPT_SUPER, 0), 0))]
        out_shape = [jax.ShapeDtypeStruct((N_PROMPT_TOK, D_MODEL), F32),
                     jax.ShapeDtypeStruct((N_TOK - N_PROMPT_TOK, D_MODEL), F32)]
    else:
        out_specs = [pl.BlockSpec((SUPER, D_MODEL), lambda s: (s, 0))]
        out_shape = [jax.ShapeDtypeStruct((N_TOK, D_MODEL), F32)]
    return pl.pallas_call(
        functools.partial(_moe_combine_kernel, final=final),
        grid=(N_SUPER,),
        in_specs=[
            pl.BlockSpec((SUPER, D_MODEL), lambda s: (s, 0)),
            pl.BlockSpec((N_EXPERTS, SLOTS, D_MODEL), lambda s: (0, s, 0)),
            pl.BlockSpec((SUPER, LANES), lambda s: (s, 0)),
            pl.BlockSpec((SUPER, LANES), lambda s: (s, 0)),
            pl.BlockSpec((N_EXPERTS * SEQ_SLOTS // LANES, LANES, LANES), lambda s: (0, 0, 0)),
            _mod_spec(layer, 1, lambda s: s),
        ],
        out_specs=out_specs,
        out_shape=out_shape,
        scratch_shapes=[pltpu.VMEM((SUPER, N_EXPERTS * SEQ_SLOTS), BF16)],
        compiler_params=_params(("arbitrary",)),
        name="moe_combine",
    )(x, ys, slot_t, gate_t, spread, mod)


def _rope_tables():
    t = np.arange(DEC_SEQ)
    row = (t // GRID_W).astype(np.float32)
    col = (t % GRID_W).astype(np.float32)
    inv = (1.0 / (ROPE_THETA ** (np.arange(0, ROPE_AXIS_DIM, 2, dtype=np.float32) / ROPE_AXIS_DIM))).astype(np.float32)
    d = np.arange(LANES) % HEAD_DIM
    use_col = (d // ROPE_AXIS_DIM) == 1
    second_half = ((d % ROPE_AXIS_DIM) // (ROPE_AXIS_DIM // 2)) == 1
    ang = (np.where(use_col[None, :], col[:, None], row[:, None]) * inv[d % (ROPE_AXIS_DIM // 2)][None, :]).astype(np.float32)
    cos = np.cos(ang).astype(np.float32)
    sin = (np.where(second_half[None, :], 1.0, -1.0) * np.sin(ang)).astype(np.float32)
    return jnp.asarray(cos), jnp.asarray(sin)


def _head_block_ones():
    lane = np.arange(LANES)
    return jnp.asarray((lane[:, None] // HEAD_DIM) == (lane[None, :] // HEAD_DIM), dtype=BF16)


def _spread_tables():
    tiles = N_EXPERTS * SEQ_SLOTS // LANES
    e = np.arange(LANES)[None, :, None]
    lane = np.arange(LANES)[None, None, :]
    tile = np.arange(tiles)[:, None, None]
    return jnp.asarray(e == tile * (LANES // SEQ_SLOTS) + lane // SEQ_SLOTS, dtype=BF16)


def kernel(x_prompt, x_sample, cache_k, cache_v, c, c_ctx, w_mod, b_mod, norm_mix, norm_ffn, attn_w_qkv, attn_q_norm, attn_k_norm, attn_w_o, conv_w_pw1, conv_b_pw1, conv_w_dw, conv_b_dw, conv_ln_g, conv_ln_b, conv_w_pw2, conv_b_pw2, moe_w_router, moe_w_gate, moe_w_up, moe_w_down):
    xs = [x_prompt.reshape(N_PROMPT_TOK, D_MODEL), x_sample.reshape(N_TOK - N_PROMPT_TOK, D_MODEL)]
    cond8 = jnp.concatenate([jnp.broadcast_to(c_ctx[None, :], (N_PROMPT_SUPER, D_MODEL)), c], axis=0)
    mod = _adaln(cond8, w_mod, b_mod)

    cos_t, sin_t = _rope_tables()
    bd = _head_block_ones()
    spread = _spread_tables()
    wr_t = jnp.swapaxes(moe_w_router, 1, 2)
    cache_k4 = cache_k.reshape(DEC_BATCH, -1, PAST_LEN, KV_DIM)
    cache_v4 = cache_v.reshape(DEC_BATCH, -1, PAST_LEN, KV_DIM)
    w_o5 = attn_w_o.reshape(-1, N_KV_HEADS, N_GROUPS, HEAD_DIM, D_MODEL)

    new_k, new_v = [], []
    for i in range(DEPTH):
        j = i // N_MIXERS
        g_mix = norm_mix[i][None, :]
        if i % N_MIXERS == 0:
            qg2 = jnp.tile(attn_q_norm[j], LANES // HEAD_DIM)[None, :]
            kg2 = jnp.tile(attn_k_norm[j], LANES // HEAD_DIM)[None, :]
            q, k, v, kc, vc = _qkv(xs, mod, i, g_mix, attn_w_qkv, j, qg2, kg2, cos_t, sin_t, bd)
            new_k.append(kc[:N_PROMPT_TOK].reshape(BATCH, SEQ, N_KV_HEADS, HEAD_DIM))
            new_v.append(vc[:N_PROMPT_TOK].reshape(BATCH, SEQ, N_KV_HEADS, HEAD_DIM))
            o_p, o_s = _attention(q, k, v, cache_k4, cache_v4, j)
            x = _oproj(xs, o_p, o_s, w_o5, j, mod, i)
        else:
            u = _conv_glu(xs[0], mod, i, g_mix, conv_w_pw1, conv_b_pw1[:, None, :], j)
            x = _conv_tail(xs[0], u, conv_w_dw, conv_b_dw[:, None, :], conv_ln_g[:, None, :], conv_ln_b[:, None, :],
                           conv_w_pw2, conv_b_pw2[:, None, :], mod, i, j)
        xg, slot_t, gate_t = _moe_select(x, mod, i, norm_ffn[i][None, :], wr_t)
        ys = _moe_ffn(xg, moe_w_gate, moe_w_up, moe_w_down, i)
        xs = _moe_combine(x, ys, slot_t, gate_t, spread, mod, i, final=(i == DEPTH - 1))

    y_prompt = xs[0].reshape(BATCH, SEQ, D_MODEL)
    y_sample = xs[1].reshape(DEC_BATCH, DEC_SEQ, D_MODEL)
    return (y_prompt, y_sample, jnp.stack(new_k, axis=1), jnp.stack(new_v, axis=1))
```

```python
import functools

import numpy as np

import jax
import jax.numpy as jnp
from jax import lax
from jax.experimental import pallas as pl
from jax.experimental.pallas import tpu as pltpu

F32 = jnp.float32
BF16 = jnp.bfloat16
I32 = jnp.int32

D_MODEL = 1024
BATCH = 16
SEQ = 256
DEPTH = 4
DEC_BATCH = 4
DEC_SEQ = 1024
PAST_LEN = 512
GRID_W = 64
N_MIXERS = 2
HEAD_DIM = 64
N_HEADS = 16
N_KV_HEADS = 4
N_GROUPS = N_HEADS // N_KV_HEADS
KV_DIM = N_KV_HEADS * HEAD_DIM
QKV_DIM = D_MODEL + 2 * KV_DIM
ROPE_AXIS_DIM = HEAD_DIM // 2
ROPE_THETA = 10000.0
CONV_K = 31
N_EXPERTS = 16
EXPERT_FF = 2 * D_MODEL
N_MOD = 6
EPS = 1e-6

N_PROMPT_TOK = BATCH * SEQ
N_TOK = N_PROMPT_TOK + DEC_BATCH * DEC_SEQ
SUPER = 1024
N_SUPER = N_TOK // SUPER
N_PROMPT_SUPER = N_PROMPT_TOK // SUPER
SEQ_PER_SUPER = SUPER // SEQ
SLOTS = SUPER // 8
SEQ_SLOTS = SEQ // 8
LANES = 128
TILE_M = 1024
N_TILES = N_TOK // TILE_M
N_PROMPT_TILES = N_PROMPT_TOK // TILE_M
QKV_SUB = 256
PROJ_SUB = 256
ATTN_TQ = 512
PROMPT_ATTN_SEQS = 4
MOD_COLS = 2 * D_MODEL
FF_CHUNK = 1024
N_FF_CHUNKS = EXPERT_FF // FF_CHUNK
CONV_ROWS = 64
CONV_HALO = 16
CONV_TAIL_ROWS = 256
GATHER_EXPERTS = 4
SCATTER_EXPERTS = 2
VMEM_LIMIT = 56 * 1024 * 1024


def _dot(a, b):
    return jnp.dot(a, b, preferred_element_type=F32)


def _dot_nt(a, b):
    return lax.dot_general(a, b, (((1,), (1,)), ((), ())), preferred_element_type=F32)


def _sigmoid(x):
    return 1.0 / (1.0 + jnp.exp(-x))


def _norm_mod(x, g, shift, scale):
    y = x * lax.rsqrt(jnp.mean(x * x, axis=-1, keepdims=True) + EPS) * g
    return y * (1.0 + scale) + shift


def _params(semantics):
    return pltpu.CompilerParams(dimension_semantics=semantics, vmem_limit_bytes=VMEM_LIMIT)


def _mod_spec(layer, which, super_of):
    return pl.BlockSpec((1, 1, 3, 1, 1, D_MODEL), lambda *idx: (layer, which, 0, super_of(*idx), 0, 0))


def _tile_super(i):
    return i // (SUPER // TILE_M)


def _x_specs(n_x, rows, n_prompt_blocks):
    if n_x == 1:
        return [pl.BlockSpec((rows, D_MODEL), lambda i: (i, 0))]
    return [pl.BlockSpec((rows, D_MODEL), lambda i: (jnp.minimum(i, n_prompt_blocks - 1), 0)),
            pl.BlockSpec((rows, D_MODEL), lambda i: (jnp.maximum(i - n_prompt_blocks, 0), 0))]


def _read_x(x_refs, i, n_prompt_blocks):
    if len(x_refs) == 1:
        return x_refs[0][...]
    return jnp.where(i < n_prompt_blocks, x_refs[0][...], x_refs[1][...])


def _adaln_kernel(cond_ref, w_ref, b_ref, o_ref):
    c = cond_ref[...]
    s = (c * _sigmoid(c)).astype(BF16)
    res = _dot(s, w_ref[0].astype(BF16))
    for v in range(MOD_COLS // D_MODEL):
        o_ref[0, v] = res[:, v * D_MODEL:(v + 1) * D_MODEL] + b_ref[0, v]


def _adaln(cond8, w_mod, b_mod):
    b4 = b_mod.reshape(DEPTH, N_MOD, 1, D_MODEL)
    per_step = MOD_COLS // D_MODEL
    out = pl.pallas_call(
        _adaln_kernel,
        grid=(DEPTH, N_MOD // per_step),
        in_specs=[
            pl.BlockSpec((N_SUPER, D_MODEL), lambda l, n: (0, 0)),
            pl.BlockSpec((1, D_MODEL, MOD_COLS), lambda l, n: (l, 0, n)),
            pl.BlockSpec((1, per_step, 1, D_MODEL), lambda l, n: (l, n, 0, 0)),
        ],
        out_specs=pl.BlockSpec((1, per_step, N_SUPER, D_MODEL), lambda l, n: (l, n, 0, 0)),
        out_shape=jax.ShapeDtypeStruct((DEPTH, N_MOD, N_SUPER, D_MODEL), F32),
        compiler_params=_params(("arbitrary", "arbitrary")),
        name="adaln",
    )(cond8, w_mod, b4)
    return out.reshape(DEPTH, 2, 3, N_SUPER, 1, D_MODEL)


def _qkv_kernel(*refs, n_x):
    x_refs = refs[:n_x]
    (m_ref, g_ref, w_ref, qg_ref, kg_ref, cos_ref, sin_ref, bd_ref,
     q_ref, k_ref, v_ref, kc_ref, vc_ref, wbf_ref) = refs[n_x:]
    i = pl.program_id(0)

    @pl.when(i == 0)
    def _():
        wbf_ref[...] = w_ref[0].astype(BF16)

    bd = bd_ref[...]
    lane = lax.broadcasted_iota(I32, (QKV_SUB, LANES), 1)
    first_half = (lane & (ROPE_AXIS_DIM // 2)) == 0
    low_head = lane < HEAD_DIM
    latent = i >= N_PROMPT_SUPER
    qg = qg_ref[...] * (HEAD_DIM ** -0.5)
    kg = kg_ref[...]
    g_mix = g_ref[...]
    shift = m_ref[0, 0, 0, 0]
    scale = m_ref[0, 0, 1, 0]

    def head_norm(t, gain):
        sq = t * t
        hi = sq.astype(BF16)
        lo = (sq - hi.astype(F32)).astype(BF16)
        ssum = _dot(hi, bd) + _dot(lo, bd)
        return t * lax.rsqrt(ssum * (1.0 / HEAD_DIM) + EPS) * gain

    for sb in range(SUPER // QKV_SUB):
        rows = pl.ds(sb * QKV_SUB, QKV_SUB)
        x = x_refs[0][rows, :] if n_x == 1 else jnp.where(latent, x_refs[1][rows, :], x_refs[0][rows, :])
        h = _norm_mod(x, g_mix, shift, scale).astype(BF16)
        qkv = _dot(h, wbf_ref[...])
        cos = cos_ref[rows, :]
        sin = sin_ref[rows, :]

        def rope(y):
            partner = jnp.where(first_half, pltpu.roll(y, LANES - ROPE_AXIS_DIM // 2, 1),
                                pltpu.roll(y, ROPE_AXIS_DIM // 2, 1))
            return jnp.where(latent, y * cos + partner * sin, y)

        q_tiles = [rope(head_norm(qkv[:, j * LANES:(j + 1) * LANES], qg)) for j in range(D_MODEL // LANES)]
        for t in range(D_MODEL // LANES):
            grp = t // 2
            src = (4 * (2 * (t % 2)) + grp) // 2
            lo_part = q_tiles[src] if grp % 2 == 0 else pltpu.roll(q_tiles[src], HEAD_DIM, 1)
            hi_part = q_tiles[src + 2] if grp % 2 == 1 else pltpu.roll(q_tiles[src + 2], HEAD_DIM, 1)
            q_ref[rows, t * LANES:(t + 1) * LANES] = jnp.where(low_head, lo_part, hi_part).astype(BF16)
        for j in range(KV_DIM // LANES):
            y = head_norm(qkv[:, D_MODEL + j * LANES:D_MODEL + (j + 1) * LANES], kg)
            k_ref[rows, j * LANES:(j + 1) * LANES] = rope(y).astype(BF16)
            kc_ref[rows, j * LANES:(j + 1) * LANES] = y
        v = qkv[:, D_MODEL + KV_DIM:]
        v_ref[rows, :] = v.astype(BF16)
        vc_ref[rows, :] = v


def _qkv(xs, mod, layer, g, w_qkv, j, qg2, kg2, cos_t, sin_t, bd):
    tile = lambda i: (i, 0)
    const = lambda i: (0, 0)
    return pl.pallas_call(
        functools.partial(_qkv_kernel, n_x=len(xs)),
        grid=(N_SUPER,),
        in_specs=_x_specs(len(xs), SUPER, N_PROMPT_SUPER) + [
            _mod_spec(layer, 0, lambda i: i),
            pl.BlockSpec((1, D_MODEL), const),
            pl.BlockSpec((1, D_MODEL, QKV_DIM), lambda i: (j, 0, 0)),
            pl.BlockSpec((1, LANES), const),
            pl.BlockSpec((1, LANES), const),
            pl.BlockSpec((DEC_SEQ, LANES), const),
            pl.BlockSpec((DEC_SEQ, LANES), const),
            pl.BlockSpec((LANES, LANES), const),
        ],
        out_specs=[
            pl.BlockSpec((SUPER, D_MODEL), tile),
            pl.BlockSpec((SUPER, KV_DIM), tile),
            pl.BlockSpec((SUPER, KV_DIM), tile),
            pl.BlockSpec((SUPER, KV_DIM), tile),
            pl.BlockSpec((SUPER, KV_DIM), tile),
        ],
        out_shape=[
            jax.ShapeDtypeStruct((N_TOK, D_MODEL), BF16),
            jax.ShapeDtypeStruct((N_TOK, KV_DIM), BF16),
            jax.ShapeDtypeStruct((N_TOK, KV_DIM), BF16),
            jax.ShapeDtypeStruct((N_TOK, KV_DIM), F32),
            jax.ShapeDtypeStruct((N_TOK, KV_DIM), F32),
        ],
        scratch_shapes=[pltpu.VMEM((D_MODEL, QKV_DIM), BF16)],
        compiler_params=_params(("arbitrary",)),
        name="qkv_proj",
    )(*xs, mod, g, w_qkv, qg2, kg2, cos_t, sin_t, bd)


def _attn_heads(q_ref, rows, n_rows, k_all, v_all, o_ref):
    lane = lax.broadcasted_iota(I32, (n_rows, KV_DIM), 1)
    for g in range(N_GROUPS):
        qs = q_ref[rows, g * KV_DIM:(g + 1) * KV_DIM]
        acc = jnp.zeros((n_rows, KV_DIM), F32)
        for kv in range(N_KV_HEADS):
            mask = (lane >= kv * HEAD_DIM) & (lane < (kv + 1) * HEAD_DIM)
            s = _dot_nt(jnp.where(mask, qs, jnp.zeros_like(qs)), k_all)
            p = jnp.exp(s - jnp.max(s, axis=-1, keepdims=True))
            inv = 1.0 / jnp.sum(p, axis=-1, keepdims=True)
            o = _dot(p.astype(BF16), v_all)
            acc = acc + jnp.where(mask, o * inv, 0.0)
        o_ref[rows, g * KV_DIM:(g + 1) * KV_DIM] = acc.astype(BF16)


def _attn_prompt_kernel(q_ref, k_ref, v_ref, o_ref):
    for r in range(PROMPT_ATTN_SEQS):
        rows = pl.ds(r * SEQ, SEQ)
        _attn_heads(q_ref, rows, SEQ, k_ref[rows, :], v_ref[rows, :], o_ref)


def _attn_sample_kernel(q_ref, k_ref, v_ref, ck_ref, cv_ref, o_ref, kall_ref, vall_ref):
    @pl.when(pl.program_id(1) == 0)
    def _():
        kall_ref[:PAST_LEN] = ck_ref[0, 0].astype(BF16)
        kall_ref[PAST_LEN:] = k_ref[...]
        vall_ref[:PAST_LEN] = cv_ref[0, 0].astype(BF16)
        vall_ref[PAST_LEN:] = v_ref[...]

    _attn_heads(q_ref, pl.ds(0, ATTN_TQ), ATTN_TQ, kall_ref[...], vall_ref[...], o_ref)


def _attention(q, k, v, cache_k4, cache_v4, j):
    p_rows = PROMPT_ATTN_SEQS * SEQ
    o_p = pl.pallas_call(
        _attn_prompt_kernel,
        grid=(BATCH // PROMPT_ATTN_SEQS,),
        in_specs=[
            pl.BlockSpec((p_rows, D_MODEL), lambda b: (b, 0)),
            pl.BlockSpec((p_rows, KV_DIM), lambda b: (b, 0)),
            pl.BlockSpec((p_rows, KV_DIM), lambda b: (b, 0)),
        ],
        out_specs=pl.BlockSpec((p_rows, D_MODEL), lambda b: (b, 0)),
        out_shape=jax.ShapeDtypeStruct((N_PROMPT_TOK, D_MODEL), BF16),
        compiler_params=_params(("arbitrary",)),
        name="attn_prompt",
    )(q, k, v)
    q_blocks = DEC_SEQ // ATTN_TQ
    first_q = N_PROMPT_TOK // ATTN_TQ
    first_k = N_PROMPT_TOK // DEC_SEQ
    o_s = pl.pallas_call(
        _attn_sample_kernel,
        grid=(DEC_BATCH, q_blocks),
        in_specs=[
            pl.BlockSpec((ATTN_TQ, D_MODEL), lambda b, t: (first_q + b * q_blocks + t, 0)),
            pl.BlockSpec((DEC_SEQ, KV_DIM), lambda b, t: (first_k + b, 0)),
            pl.BlockSpec((DEC_SEQ, KV_DIM), lambda b, t: (first_k + b, 0)),
            pl.BlockSpec((1, 1, PAST_LEN, KV_DIM), lambda b, t: (b, j, 0, 0)),
            pl.BlockSpec((1, 1, PAST_LEN, KV_DIM), lambda b, t: (b, j, 0, 0)),
        ],
        out_specs=pl.BlockSpec((ATTN_TQ, D_MODEL), lambda b, t: (b * q_blocks + t, 0)),
        out_shape=jax.ShapeDtypeStruct((DEC_BATCH * DEC_SEQ, D_MODEL), BF16),
        scratch_shapes=[pltpu.VMEM((PAST_LEN + DEC_SEQ, KV_DIM), BF16),
                        pltpu.VMEM((PAST_LEN + DEC_SEQ, KV_DIM), BF16)],
        compiler_params=_params(("arbitrary", "arbitrary")),
        name="attn_sample",
    )(q, k, v, cache_k4, cache_v4)
    return o_p, o_s


def _oproj_kernel(*refs, n_x):
    x_refs = refs[:n_x]
    op_ref, os_ref = refs[n_x:n_x + 2]
    w_refs = refs[n_x + 2:n_x + 2 + N_GROUPS]
    m_ref, mf_ref, gf_ref, out_ref, h_ref, wbf_ref = refs[n_x + 2 + N_GROUPS:]
    i = pl.program_id(0)

    @pl.when(i == 0)
    def _():
        for g in range(N_GROUPS):
            wbf_ref[g * KV_DIM:(g + 1) * KV_DIM, :] = w_refs[g][0, :, 0].reshape(KV_DIM, D_MODEL).astype(BF16)

    prompt = i < N_PROMPT_TILES
    gate = m_ref[0, 0, 2, 0]
    g_ffn = gf_ref[...]
    shift_f = mf_ref[0, 0, 0, 0]
    scale_f = mf_ref[0, 0, 1, 0]
    for sb in range(TILE_M // PROJ_SUB):
        rows = pl.ds(sb * PROJ_SUB, PROJ_SUB)
        o = jnp.where(prompt, op_ref[rows, :], os_ref[rows, :])
        x = x_refs[0][rows, :] if n_x == 1 else jnp.where(prompt, x_refs[0][rows, :], x_refs[1][rows, :])
        x1 = x + gate * _dot(o, wbf_ref[...])
        out_ref[rows, :] = x1
        h_ref[rows, :] = _norm_mod(x1, g_ffn, shift_f, scale_f).astype(BF16)


def _oproj(xs, o_p, o_s, w_o5, j, mod, layer, g_ffn):
    w_specs = [pl.BlockSpec((1, N_KV_HEADS, 1, HEAD_DIM, D_MODEL), lambda i, g=g: (j, 0, g, 0, 0))
               for g in range(N_GROUPS)]
    return pl.pallas_call(
        functools.partial(_oproj_kernel, n_x=len(xs)),
        grid=(N_TILES,),
        in_specs=_x_specs(len(xs), TILE_M, N_PROMPT_TILES) + [
            pl.BlockSpec((TILE_M, D_MODEL), lambda i: (jnp.minimum(i, N_PROMPT_TILES - 1), 0)),
            pl.BlockSpec((TILE_M, D_MODEL), lambda i: (jnp.maximum(i - N_PROMPT_TILES, 0), 0)),
        ] + w_specs + [_mod_spec(layer, 0, _tile_super), _mod_spec(layer, 1, _tile_super),
                       pl.BlockSpec((1, D_MODEL), lambda i: (0, 0))],
        out_specs=[pl.BlockSpec((TILE_M, D_MODEL), lambda i: (i, 0)),
                   pl.BlockSpec((TILE_M, D_MODEL), lambda i: (i, 0))],
        out_shape=[jax.ShapeDtypeStruct((N_TOK, D_MODEL), F32),
                   jax.ShapeDtypeStruct((N_TOK, D_MODEL), BF16)],
        scratch_shapes=[pltpu.VMEM((D_MODEL, D_MODEL), BF16)],
        compiler_params=_params(("arbitrary",)),
        name="attn_out_proj",
    )(*xs, o_p, o_s, *([w_o5] * N_GROUPS), mod, mod, g_ffn)


def _conv_glu_kernel(x_ref, m_ref, g_ref, w_ref, b_ref, u_ref, wbf_ref):
    @pl.when(pl.program_id(0) == 0)
    def _():
        wbf_ref[...] = w_ref[0].astype(BF16)

    h = _norm_mod(x_ref[...], g_ref[...], m_ref[0, 0, 0, 0], m_ref[0, 0, 1, 0]).astype(BF16)
    z = _dot(h, wbf_ref[...]) + b_ref[0]
    u_ref[...] = z[:, :D_MODEL] * _sigmoid(z[:, D_MODEL:])


def _conv_glu(x, mod, layer, g, w1, b1, j):
    return pl.pallas_call(
        _conv_glu_kernel,
        grid=(N_TILES,),
        in_specs=[
            pl.BlockSpec((TILE_M, D_MODEL), lambda i: (i, 0)),
            _mod_spec(layer, 0, _tile_super),
            pl.BlockSpec((1, D_MODEL), lambda i: (0, 0)),
            pl.BlockSpec((1, D_MODEL, 2 * D_MODEL), lambda i: (j, 0, 0)),
            pl.BlockSpec((1, 1, 2 * D_MODEL), lambda i: (j, 0, 0)),
        ],
        out_specs=pl.BlockSpec((TILE_M, D_MODEL), lambda i: (i, 0)),
        out_shape=jax.ShapeDtypeStruct((N_TOK, D_MODEL), F32),
        scratch_shapes=[pltpu.VMEM((D_MODEL, 2 * D_MODEL), BF16)],
        compiler_params=_params(("arbitrary",)),
        name="conv_glu",
    )(x, mod, g, w1, b1)


PAD_ROWS = CONV_HALO + SEQ_PER_SUPER * (SEQ + CONV_HALO)


def _conv_tail_kernel(x_ref, u_ref, wdw_ref, bdw_ref, lng_ref, lnb_ref, w2_ref, b2_ref, m_ref, mf_ref, gf_ref,
                      out_ref, h_ref, pad_ref, cv_ref, wbf_ref):
    ss = pl.program_id(0)

    @pl.when(ss == 0)
    def _():
        wbf_ref[...] = w2_ref[0].astype(BF16)

    seq_gap = jnp.where(ss < N_PROMPT_SUPER, CONV_HALO, 0)
    pad_ref[...] = jnp.zeros_like(pad_ref)
    for s in range(SEQ_PER_SUPER):
        for j in range(D_MODEL // LANES):
            pad_ref[j, pl.ds(CONV_HALO + s * SEQ + s * seq_gap, SEQ), :] = (
                u_ref[pl.ds(s * SEQ, SEQ), j * LANES:(j + 1) * LANES])

    def block(c, carry):
        r0 = pl.multiple_of(c * CONV_ROWS, CONV_ROWS)
        base = r0 + (c // (SEQ // CONV_ROWS)) * seq_gap + (CONV_HALO - CONV_K // 2)
        for j in range(D_MODEL // LANES):
            cols = slice(j * LANES, (j + 1) * LANES)
            acc = jnp.zeros((CONV_ROWS, LANES), F32)
            for kk in range(CONV_K):
                acc = acc + wdw_ref[0, pl.ds(kk, 1), cols] * pad_ref[j, pl.ds(base + kk, CONV_ROWS), :]
            cv_ref[pl.ds(r0, CONV_ROWS), cols] = acc + bdw_ref[0, :, cols]
        return carry

    lax.fori_loop(0, SUPER // CONV_ROWS, block, 0)

    gate = m_ref[0, 0, 2, 0]
    for sb in range(SUPER // CONV_TAIL_ROWS):
        rows = pl.ds(sb * CONV_TAIL_ROWS, CONV_TAIL_ROWS)
        cv = cv_ref[rows, :]
        mu = jnp.mean(cv, axis=-1, keepdims=True)
        cen = cv - mu
        var = jnp.mean(cen * cen, axis=-1, keepdims=True)
        y = cen * lax.rsqrt(var + EPS) * lng_ref[0] + lnb_ref[0]
        act = (y * _sigmoid(y)).astype(BF16)
        x1 = x_ref[rows, :] + gate * (_dot(act, wbf_ref[...]) + b2_ref[0])
        out_ref[rows, :] = x1
        h_ref[rows, :] = _norm_mod(x1, gf_ref[...], mf_ref[0, 0, 0, 0], mf_ref[0, 0, 1, 0]).astype(BF16)


def _conv_tail(x, u, wdw, bdw, lng, lnb, w2, b2, mod, layer, j, g_ffn):
    vec = pl.BlockSpec((1, 1, D_MODEL), lambda s: (j, 0, 0))
    return pl.pallas_call(
        _conv_tail_kernel,
        grid=(N_SUPER,),
        in_specs=[
            pl.BlockSpec((SUPER, D_MODEL), lambda s: (s, 0)),
            pl.BlockSpec((SUPER, D_MODEL), lambda s: (s, 0)),
            pl.BlockSpec((1, CONV_K, D_MODEL), lambda s: (j, 0, 0)),
            vec, vec, vec,
            pl.BlockSpec((1, D_MODEL, D_MODEL), lambda s: (j, 0, 0)),
            vec,
            _mod_spec(layer, 0, lambda s: s),
            _mod_spec(layer, 1, lambda s: s),
            pl.BlockSpec((1, D_MODEL), lambda s: (0, 0)),
        ],
        out_specs=[pl.BlockSpec((SUPER, D_MODEL), lambda s: (s, 0)),
                   pl.BlockSpec((SUPER, D_MODEL), lambda s: (s, 0))],
        out_shape=[jax.ShapeDtypeStruct((N_TOK, D_MODEL), F32),
                   jax.ShapeDtypeStruct((N_TOK, D_MODEL), BF16)],
        scratch_shapes=[pltpu.VMEM((D_MODEL // LANES, PAD_ROWS, LANES), F32),
                        pltpu.VMEM((SUPER, D_MODEL), F32),
                        pltpu.VMEM((D_MODEL, D_MODEL), BF16)],
        compiler_params=_params(("arbitrary",)),
        name="conv_tail",
    )(x, u, wdw, bdw, lng, lnb, w2, b2, mod, mod, g_ffn)


def _route(aff, cap, tri):
    rows = aff.shape[0]
    capf = float(cap)

    def count_ge(bits):
        return jnp.sum((aff >= lax.bitcast_convert_type(bits, F32)).astype(F32), axis=1, keepdims=True)

    thr = jnp.zeros((rows, 1), I32)
    top = thr | (1 << 30)
    thr = jnp.where(count_ge(top) >= capf, top, thr)
    for shift in range(27, -1, -3):
        digit = jnp.zeros((rows, 1), I32)
        for d in range(1, 8):
            digit = digit + (count_ge(thr | (d << shift)) >= capf).astype(I32)
        thr = thr | (digit << shift)
    thr_f = lax.bitcast_convert_type(thr, F32)
    above = aff > thr_f
    tied = aff == thr_f
    need = capf - jnp.sum(above.astype(F32), axis=1, keepdims=True)
    tied_rank = _dot(tied.astype(BF16), tri)
    sel = above | (tied & (tied_rank <= need))
    return jnp.where(sel, _dot(sel.astype(BF16), tri) - 1.0, -1.0)


def _pad_rows_t(a, fill):
    pad = jnp.full((LANES - N_EXPERTS, a.shape[1]), fill, F32)
    return jnp.concatenate([a, pad], axis=0).T


def _moe_select_kernel(h_ref, wrt_ref, xs_ref, slot_ref, gate_ref, tri_ref, p_ref):
    ss = pl.program_id(0)

    @pl.when(ss == 0)
    def _():
        r = lax.broadcasted_iota(I32, (SUPER, SUPER), 0)
        c = lax.broadcasted_iota(I32, (SUPER, SUPER), 1)
        tri_ref[...] = (r <= c).astype(BF16)

    logit = _dot_nt(wrt_ref[0].astype(BF16), h_ref[...])
    ex = jnp.exp(logit - jnp.max(logit, axis=0, keepdims=True))
    aff = ex / jnp.sum(ex, axis=0, keepdims=True)

    @pl.when(ss < N_PROMPT_SUPER)
    def _():
        aff_r = jnp.concatenate([aff[:, b * SEQ:(b + 1) * SEQ] for b in range(SEQ_PER_SUPER)], axis=0)
        pos = _route(aff_r, SEQ_SLOTS, tri_ref[:SEQ, :SEQ])
        sub = lax.broadcasted_iota(I32, (SEQ_SLOTS, SEQ), 0).astype(F32)
        for b in range(SEQ_PER_SUPER):
            for e in range(N_EXPERTS):
                r = b * N_EXPERTS + e
                p_ref[e * SEQ_SLOTS:(e + 1) * SEQ_SLOTS, :] = (sub == pos[r:r + 1, :]).astype(BF16)
            rows = _dot(p_ref[...], h_ref[b * SEQ:(b + 1) * SEQ, :])
            xs_ref[:, b * SEQ_SLOTS:(b + 1) * SEQ_SLOTS, :] = (
                rows.astype(BF16).reshape(N_EXPERTS, SEQ_SLOTS, D_MODEL))
            slot_ref[b * SEQ:(b + 1) * SEQ, :] = _pad_rows_t(pos[b * N_EXPERTS:(b + 1) * N_EXPERTS], -1.0)
            gate_ref[b * SEQ:(b + 1) * SEQ, :] = _pad_rows_t(aff[:, b * SEQ:(b + 1) * SEQ], 0.0)

    @pl.when(ss >= N_PROMPT_SUPER)
    def _():
        pos = _route(aff, SLOTS, tri_ref[...])
        sub = lax.broadcasted_iota(I32, (SLOTS, SUPER), 0).astype(F32)
        for e0 in range(0, N_EXPERTS, GATHER_EXPERTS):
            onehot = jnp.concatenate([(sub == pos[e:e + 1, :]).astype(BF16)
                                      for e in range(e0, e0 + GATHER_EXPERTS)], axis=0)
            rows = _dot(onehot, h_ref[...])
            xs_ref[e0:e0 + GATHER_EXPERTS] = rows.astype(BF16).reshape(GATHER_EXPERTS, SLOTS, D_MODEL)
        slot_ref[...] = _pad_rows_t(pos, -1.0)
        gate_ref[...] = _pad_rows_t(aff, 0.0)


def _moe_select(h, layer, wr_t):
    return pl.pallas_call(
        _moe_select_kernel,
        grid=(N_SUPER,),
        in_specs=[
            pl.BlockSpec((SUPER, D_MODEL), lambda s: (s, 0)),
            pl.BlockSpec((1, N_EXPERTS, D_MODEL), lambda s: (layer, 0, 0)),
        ],
        out_specs=[
            pl.BlockSpec((N_EXPERTS, SLOTS, D_MODEL), lambda s: (0, s, 0)),
            pl.BlockSpec((SUPER, LANES), lambda s: (s, 0)),
            pl.BlockSpec((SUPER, LANES), lambda s: (s, 0)),
        ],
        out_shape=[
            jax.ShapeDtypeStruct((N_EXPERTS, N_SUPER * SLOTS, D_MODEL), BF16),
            jax.ShapeDtypeStruct((N_TOK, LANES), F32),
            jax.ShapeDtypeStruct((N_TOK, LANES), F32),
        ],
        scratch_shapes=[pltpu.VMEM((SUPER, SUPER), BF16),
                        pltpu.VMEM((N_EXPERTS * SEQ_SLOTS, SEQ), BF16)],
        compiler_params=_params(("arbitrary",)),
        name="moe_select",
    )(h, wr_t)


def _moe_ffn_kernel(x_ref, wg_ref, wu_ref, wd_ref, y_ref, acc_ref):
    f = pl.program_id(1)
    x = x_ref[0]
    a = _dot(x, wg_ref[0, 0].astype(BF16))
    u = _dot(x, wu_ref[0, 0].astype(BF16))
    hid = (a * _sigmoid(a) * u).astype(BF16)
    acc_ref[f] = _dot(hid, wd_ref[0, 0].astype(BF16))

    @pl.when(f == N_FF_CHUNKS - 1)
    def _():
        y_ref[0] = (acc_ref[0] + acc_ref[1]).astype(BF16)


def _moe_ffn(xs, w_gate, w_up, w_down, layer):
    assert N_FF_CHUNKS == 2
    rows = N_SUPER * SLOTS
    return pl.pallas_call(
        _moe_ffn_kernel,
        grid=(N_EXPERTS, N_FF_CHUNKS),
        in_specs=[
            pl.BlockSpec((1, rows, D_MODEL), lambda e, f: (e, 0, 0)),
            pl.BlockSpec((1, 1, D_MODEL, FF_CHUNK), lambda e, f: (layer, e, 0, f)),
            pl.BlockSpec((1, 1, D_MODEL, FF_CHUNK), lambda e, f: (layer, e, 0, f)),
            pl.BlockSpec((1, 1, FF_CHUNK, D_MODEL), lambda e, f: (layer, e, f, 0)),
        ],
        out_specs=pl.BlockSpec((1, rows, D_MODEL), lambda e, f: (e, 0, 0)),
        out_shape=jax.ShapeDtypeStruct((N_EXPERTS, rows, D_MODEL), BF16),
        scratch_shapes=[pltpu.VMEM((N_FF_CHUNKS, rows, D_MODEL), F32)],
        compiler_params=_params(("arbitrary", "arbitrary")),
        name="moe_ffn",
    )(xs, w_gate, w_up, w_down)


def _moe_combine_kernel(x_ref, y_ref, slot_ref, gate_ref, spread_ref, m_ref, *refs, final):
    out_refs, s_ref = refs[:-1], refs[-1]
    ss = pl.program_id(0)
    gate_f = m_ref[0, 0, 2, 0]

    @pl.when(ss < N_PROMPT_SUPER)
    def _():
        lane_slot = (lax.broadcasted_iota(I32, (SEQ, LANES), 1) % SEQ_SLOTS).astype(F32)
        for b in range(SEQ_PER_SUPER):
            rows = pl.ds(b * SEQ, SEQ)
            slot_b = slot_ref[rows, :].astype(BF16)
            gate_b = gate_ref[rows, :].astype(BF16)
            for t in range(N_EXPERTS * SEQ_SLOTS // LANES):
                hit = _dot(slot_b, spread_ref[t]) == lane_slot
                s_ref[rows, t * LANES:(t + 1) * LANES] = jnp.where(hit, _dot(gate_b, spread_ref[t]), 0.0).astype(BF16)
            y_b = y_ref[:, b * SEQ_SLOTS:(b + 1) * SEQ_SLOTS, :].reshape(N_EXPERTS * SEQ_SLOTS, D_MODEL)
            res = x_ref[rows, :] + gate_f * _dot(s_ref[rows, :N_EXPERTS * SEQ_SLOTS], y_b)
            out_refs[0][rows, :] = res

    @pl.when(ss >= N_PROMPT_SUPER)
    def _():
        lane = lax.broadcasted_iota(I32, (SUPER, SLOTS), 1).astype(F32)
        moe = None
        for e0 in range(0, N_EXPERTS, SCATTER_EXPERTS):
            s_cols = jnp.concatenate(
                [jnp.where(slot_ref[:, e:e + 1] == lane, gate_ref[:, e:e + 1], 0.0).astype(BF16)
                 for e in range(e0, e0 + SCATTER_EXPERTS)], axis=1)
            part = _dot(s_cols, y_ref[e0:e0 + SCATTER_EXPERTS].reshape(SCATTER_EXPERTS * SLOTS, D_MODEL))
            moe = part if moe is None else moe + part
        out_refs[-1][...] = x_ref[...] + gate_f * moe


def _moe_combine(x, ys, slot_t, gate_t, spread, mod, layer, final):
    if final:
        out_specs = [pl.BlockSpec((SUPER, D_MODEL), lambda s: (jnp.minimum(s, N_PROMPT_SUPER - 1), 0)),
                     pl.BlockSpec((SUPER, D_MODEL), lambda s: (jnp.maximum(s - N_PROMPT_SUPER, 0), 0))]
        out_shape = [jax.ShapeDtypeStruct((N_PROMPT_TOK, D_MODEL), F32),
                     jax.ShapeDtypeStruct((N_TOK - N_PROMPT_TOK, D_MODEL), F32)]
    else:
        out_specs = [pl.BlockSpec((SUPER, D_MODEL), lambda s: (s, 0))]
        out_shape = [jax.ShapeDtypeStruct((N_TOK, D_MODEL), F32)]
    return pl.pallas_call(
        functools.partial(_moe_combine_kernel, final=final),
        grid=(N_SUPER,),
        in_specs=[
            pl.BlockSpec((SUPER, D_MODEL), lambda s: (s, 0)),
            pl.BlockSpec((N_EXPERTS, SLOTS, D_MODEL), lambda s: (0, s, 0)),
            pl.BlockSpec((SUPER, LANES), lambda s: (s, 0)),
            pl.BlockSpec((SUPER, LANES), lambda s: (s, 0)),
            pl.BlockSpec((N_EXPERTS * SEQ_SLOTS // LANES, LANES, LANES), lambda s: (0, 0, 0)),
            _mod_spec(layer, 1, lambda s: s),
        ],
        out_specs=out_specs,
        out_shape=out_shape,
        scratch_shapes=[pltpu.VMEM((SUPER, N_EXPERTS * SEQ_SLOTS), BF16)],
        compiler_params=_params(("arbitrary",)),
        name="moe_combine",
    )(x, ys, slot_t, gate_t, spread, mod)


def _rope_tables():
    t = np.arange(DEC_SEQ)
    row = (t // GRID_W).astype(np.float32)
    col = (t % GRID_W).astype(np.float32)
    inv = (1.0 / (ROPE_THETA ** (np.arange(0, ROPE_AXIS_DIM, 2, dtype=np.float32) / ROPE_AXIS_DIM))).astype(np.float32)
    d = np.arange(LANES) % HEAD_DIM
    use_col = (d // ROPE_AXIS_DIM) == 1
    second_half = ((d % ROPE_AXIS_DIM) // (ROPE_AXIS_DIM // 2)) == 1
    ang = (np.where(use_col[None, :], col[:, None], row[:, None]) * inv[d % (ROPE_AXIS_DIM // 2)][None, :]).astype(np.float32)
    cos = np.cos(ang).astype(np.float32)
    sin = (np.where(second_half[None, :], 1.0, -1.0) * np.sin(ang)).astype(np.float32)
    return jnp.asarray(cos), jnp.asarray(sin)


def _head_block_ones():
    lane = np.arange(LANES)
    return jnp.asarray((lane[:, None] // HEAD_DIM) == (lane[None, :] // HEAD_DIM), dtype=BF16)


def _spread_tables():
    tiles = N_EXPERTS * SEQ_SLOTS // LANES
    e = np.arange(LANES)[None, :, None]
    lane = np.arange(LANES)[None, None, :]
    tile = np.arange(tiles)[:, None, None]
    return jnp.asarray(e == tile * (LANES // SEQ_SLOTS) + lane // SEQ_SLOTS, dtype=BF16)


def kernel(x_prompt, x_sample, cache_k, cache_v, c, c_ctx, w_mod, b_mod, norm_mix, norm_ffn, attn_w_qkv, attn_q_norm, attn_k_norm, attn_w_o, conv_w_pw1, conv_b_pw1, conv_w_dw, conv_b_dw, conv_ln_g, conv_ln_b, conv_w_pw2, conv_b_pw2, moe_w_router, moe_w_gate, moe_w_up, moe_w_down):
    xs = [x_prompt.reshape(N_PROMPT_TOK, D_MODEL), x_sample.reshape(N_TOK - N_PROMPT_TOK, D_MODEL)]
    cond8 = jnp.concatenate([jnp.broadcast_to(c_ctx[None, :], (N_PROMPT_SUPER, D_MODEL)), c], axis=0)
    mod = _adaln(cond8, w_mod, b_mod)

    cos_t, sin_t = _rope_tables()
    bd = _head_block_ones()
    spread = _spread_tables()
    wr_t = jnp.swapaxes(moe_w_router, 1, 2)
    cache_k4 = cache_k.reshape(DEC_BATCH, -1, PAST_LEN, KV_DIM)
    cache_v4 = cache_v.reshape(DEC_BATCH, -1, PAST_LEN, KV_DIM)
    w_o5 = attn_w_o.reshape(-1, N_KV_HEADS, N_GROUPS, HEAD_DIM, D_MODEL)

    new_k, new_v = [], []
    for i in range(DEPTH):
        j = i // N_MIXERS
        g_mix = norm_mix[i][None, :]
        g_ffn = norm_ffn[i][None, :]
        if i % N_MIXERS == 0:
            qg2 = jnp.tile(attn_q_norm[j], LANES // HEAD_DIM)[None, :]
            kg2 = jnp.tile(attn_k_norm[j], LANES // HEAD_DIM)[None, :]
            q, k, v, kc, vc = _qkv(xs, mod, i, g_mix, attn_w_qkv, j, qg2, kg2, cos_t, sin_t, bd)
            new_k.append(kc[:N_PROMPT_TOK].reshape(BATCH, SEQ, N_KV_HEADS, HEAD_DIM))
            new_v.append(vc[:N_PROMPT_TOK].reshape(BATCH, SEQ, N_KV_HEADS, HEAD_DIM))
            o_p, o_s = _attention(q, k, v, cache_k4, cache_v4, j)
            x, h_ffn = _oproj(xs, o_p, o_s, w_o5, j, mod, i, g_ffn)
        else:
            u = _conv_glu(xs[0], mod, i, g_mix, conv_w_pw1, conv_b_pw1[:, None, :], j)
            x, h_ffn = _conv_tail(xs[0], u, conv_w_dw, conv_b_dw[:, None, :], conv_ln_g[:, None, :],
                                  conv_ln_b[:, None, :], conv_w_pw2, conv_b_pw2[:, None, :], mod, i, j, g_ffn)
        xg, slot_t, gate_t = _moe_select(h_ffn, i, wr_t)
        ys = _moe_ffn(xg, moe_w_gate, moe_w_up, moe_w_down, i)
        xs = _moe_combine(x, ys, slot_t, gate_t, spread, mod, i, final=(i == DEPTH - 1))

    y_prompt = xs[0].reshape(BATCH, SEQ, D_MODEL)
    y_sample = xs[1].reshape(DEC_BATCH, DEC_SEQ, D_MODEL)
    return (y_prompt, y_sample, jnp.stack(new_k, axis=1), jnp.stack(new_v, axis=1))
```

```python
import functools

import numpy as np

import jax
import jax.numpy as jnp
from jax import lax
from jax.experimental import pallas as pl
from jax.experimental.pallas import tpu as pltpu

F32 = jnp.float32
BF16 = jnp.bfloat16
I32 = jnp.int32

D_MODEL = 1024
BATCH = 16
SEQ = 256
DEPTH = 4
DEC_BATCH = 4
DEC_SEQ = 1024
PAST_LEN = 512
GRID_W = 64
N_MIXERS = 2
HEAD_DIM = 64
N_HEADS = 16
N_KV_HEADS = 4
N_GROUPS = N_HEADS // N_KV_HEADS
KV_DIM = N_KV_HEADS * HEAD_DIM
QKV_DIM = D_MODEL + 2 * KV_DIM
ROPE_AXIS_DIM = HEAD_DIM // 2
ROPE_THETA = 10000.0
CONV_K = 31
N_EXPERTS = 16
EXPERT_FF = 2 * D_MODEL
N_MOD = 6
EPS = 1e-6

N_PROMPT_TOK = BATCH * SEQ
N_TOK = N_PROMPT_TOK + DEC_BATCH * DEC_SEQ
SUPER = 1024
N_SUPER = N_TOK // SUPER
N_PROMPT_SUPER = N_PROMPT_TOK // SUPER
SEQ_PER_SUPER = SUPER // SEQ
SLOTS = SUPER // 8
SEQ_SLOTS = SEQ // 8
LANES = 128
TILE_M = 1024
N_TILES = N_TOK // TILE_M
N_PROMPT_TILES = N_PROMPT_TOK // TILE_M
QKV_SUB = 256
PROJ_SUB = 256
ATTN_TQ = 512
PROMPT_ATTN_SEQS = 4
MOD_COLS = 2 * D_MODEL
MOD_CHUNK = 256
FF_CHUNK = 1024
N_FF_CHUNKS = EXPERT_FF // FF_CHUNK
CONV_ROWS = 64
CONV_HALO = 16
CONV_TAIL_ROWS = 256
GATHER_EXPERTS = 4
SCATTER_EXPERTS = 2
VMEM_LIMIT = 56 * 1024 * 1024


def _dot(a, b):
    return jnp.dot(a, b, preferred_element_type=F32)


def _dot_nt(a, b):
    return lax.dot_general(a, b, (((1,), (1,)), ((), ())), preferred_element_type=F32)


def _sigmoid(x):
    return 1.0 / (1.0 + jnp.exp(-x))


def _norm_mod(x, g, shift, scale):
    y = x * lax.rsqrt(jnp.mean(x * x, axis=-1, keepdims=True) + EPS) * g
    return y * (1.0 + scale) + shift


def _params(semantics):
    return pltpu.CompilerParams(dimension_semantics=semantics, vmem_limit_bytes=VMEM_LIMIT)


def _mod_spec(which, super_of):
    return pl.BlockSpec((1, 3, 1, 1, D_MODEL), lambda *idx: (which, 0, super_of(*idx), 0, 0))


def _tile_super(i):
    return i // (SUPER // TILE_M)


def _x_specs(n_x, rows, n_prompt_blocks):
    if n_x == 1:
        return [pl.BlockSpec((rows, D_MODEL), lambda i: (i, 0))]
    return [pl.BlockSpec((rows, D_MODEL), lambda i: (jnp.minimum(i, n_prompt_blocks - 1), 0)),
            pl.BlockSpec((rows, D_MODEL), lambda i: (jnp.maximum(i - n_prompt_blocks, 0), 0))]


def _read_x(x_refs, i, n_prompt_blocks):
    if len(x_refs) == 1:
        return x_refs[0][...]
    return jnp.where(i < n_prompt_blocks, x_refs[0][...], x_refs[1][...])


def _adaln_kernel(cond_ref, w_ref, b_ref, o_ref):
    c = cond_ref[...]
    s = (c * _sigmoid(c)).astype(BF16)
    res = _dot(s, w_ref[0].astype(BF16))
    for v in range(MOD_COLS // D_MODEL):
        o_ref[0, v] = res[:, v * D_MODEL:(v + 1) * D_MODEL] + b_ref[0, v]


def _adaln_first(cond8, w_mod, b_mod):
    b4 = b_mod.reshape(DEPTH, N_MOD, 1, D_MODEL)
    per_step = MOD_COLS // D_MODEL
    out = pl.pallas_call(
        _adaln_kernel,
        grid=(N_MOD // per_step,),
        in_specs=[
            pl.BlockSpec((N_SUPER, D_MODEL), lambda n: (0, 0)),
            pl.BlockSpec((1, D_MODEL, MOD_COLS), lambda n: (0, 0, n)),
            pl.BlockSpec((1, per_step, 1, D_MODEL), lambda n: (0, n, 0, 0)),
        ],
        out_specs=pl.BlockSpec((1, per_step, N_SUPER, D_MODEL), lambda n: (0, n, 0, 0)),
        out_shape=jax.ShapeDtypeStruct((1, N_MOD, N_SUPER, D_MODEL), F32),
        compiler_params=_params(("arbitrary",)),
        name="adaln",
    )(cond8, w_mod, b4)
    return out.reshape(2, 3, N_SUPER, 1, D_MODEL)


def _qkv_kernel(*refs, n_x):
    x_refs = refs[:n_x]
    (m_ref, g_ref, w_ref, qg_ref, kg_ref, cos_ref, sin_ref, bd_ref,
     q_ref, k_ref, v_ref, kc_ref, vc_ref, wbf_ref) = refs[n_x:]
    i = pl.program_id(0)

    @pl.when(i == 0)
    def _():
        wbf_ref[...] = w_ref[0].astype(BF16)

    bd = bd_ref[...]
    lane = lax.broadcasted_iota(I32, (QKV_SUB, LANES), 1)
    first_half = (lane & (ROPE_AXIS_DIM // 2)) == 0
    low_head = lane < HEAD_DIM
    latent = i >= N_PROMPT_SUPER
    qg = qg_ref[...] * (HEAD_DIM ** -0.5)
    kg = kg_ref[...]
    g_mix = g_ref[...]
    shift = m_ref[0, 0, 0]
    scale = m_ref[0, 1, 0]

    def head_norm(t, gain):
        sq = t * t
        hi = sq.astype(BF16)
        lo = (sq - hi.astype(F32)).astype(BF16)
        ssum = _dot(hi, bd) + _dot(lo, bd)
        return t * lax.rsqrt(ssum * (1.0 / HEAD_DIM) + EPS) * gain

    for sb in range(SUPER // QKV_SUB):
        rows = pl.ds(sb * QKV_SUB, QKV_SUB)
        x = x_refs[0][rows, :] if n_x == 1 else jnp.where(latent, x_refs[1][rows, :], x_refs[0][rows, :])
        h = _norm_mod(x, g_mix, shift, scale).astype(BF16)
        qkv = _dot(h, wbf_ref[...])
        cos = cos_ref[rows, :]
        sin = sin_ref[rows, :]

        def rope(y):
            partner = jnp.where(first_half, pltpu.roll(y, LANES - ROPE_AXIS_DIM // 2, 1),
                                pltpu.roll(y, ROPE_AXIS_DIM // 2, 1))
            return jnp.where(latent, y * cos + partner * sin, y)

        q_tiles = [rope(head_norm(qkv[:, j * LANES:(j + 1) * LANES], qg)) for j in range(D_MODEL // LANES)]
        for t in range(D_MODEL // LANES):
            grp = t // 2
            src = (4 * (2 * (t % 2)) + grp) // 2
            lo_part = q_tiles[src] if grp % 2 == 0 else pltpu.roll(q_tiles[src], HEAD_DIM, 1)
            hi_part = q_tiles[src + 2] if grp % 2 == 1 else pltpu.roll(q_tiles[src + 2], HEAD_DIM, 1)
            q_ref[rows, t * LANES:(t + 1) * LANES] = jnp.where(low_head, lo_part, hi_part).astype(BF16)
        for j in range(KV_DIM // LANES):
            y = head_norm(qkv[:, D_MODEL + j * LANES:D_MODEL + (j + 1) * LANES], kg)
            k_ref[rows, j * LANES:(j + 1) * LANES] = rope(y).astype(BF16)
            kc_ref[rows, j * LANES:(j + 1) * LANES] = y
        v = qkv[:, D_MODEL + KV_DIM:]
        v_ref[rows, :] = v.astype(BF16)
        vc_ref[rows, :] = v


def _qkv(xs, mod, g, w_qkv, j, qg2, kg2, cos_t, sin_t, bd):
    tile = lambda i: (i, 0)
    const = lambda i: (0, 0)
    return pl.pallas_call(
        functools.partial(_qkv_kernel, n_x=len(xs)),
        grid=(N_SUPER,),
        in_specs=_x_specs(len(xs), SUPER, N_PROMPT_SUPER) + [
            _mod_spec(0, lambda i: i),
            pl.BlockSpec((1, D_MODEL), const),
            pl.BlockSpec((1, D_MODEL, QKV_DIM), lambda i: (j, 0, 0)),
            pl.BlockSpec((1, LANES), const),
            pl.BlockSpec((1, LANES), const),
            pl.BlockSpec((DEC_SEQ, LANES), const),
            pl.BlockSpec((DEC_SEQ, LANES), const),
            pl.BlockSpec((LANES, LANES), const),
        ],
        out_specs=[
            pl.BlockSpec((SUPER, D_MODEL), tile),
            pl.BlockSpec((SUPER, KV_DIM), tile),
            pl.BlockSpec((SUPER, KV_DIM), tile),
            pl.BlockSpec((SUPER, KV_DIM), tile),
            pl.BlockSpec((SUPER, KV_DIM), tile),
        ],
        out_shape=[
            jax.ShapeDtypeStruct((N_TOK, D_MODEL), BF16),
            jax.ShapeDtypeStruct((N_TOK, KV_DIM), BF16),
            jax.ShapeDtypeStruct((N_TOK, KV_DIM), BF16),
            jax.ShapeDtypeStruct((N_TOK, KV_DIM), F32),
            jax.ShapeDtypeStruct((N_TOK, KV_DIM), F32),
        ],
        scratch_shapes=[pltpu.VMEM((D_MODEL, QKV_DIM), BF16)],
        compiler_params=_params(("arbitrary",)),
        name="qkv_proj",
    )(*xs, mod, g, w_qkv, qg2, kg2, cos_t, sin_t, bd)


def _attn_heads(q_ref, rows, n_rows, k_all, v_all, o_ref):
    lane = lax.broadcasted_iota(I32, (n_rows, KV_DIM), 1)
    for g in range(N_GROUPS):
        qs = q_ref[rows, g * KV_DIM:(g + 1) * KV_DIM]
        acc = jnp.zeros((n_rows, KV_DIM), F32)
        for kv in range(N_KV_HEADS):
            mask = (lane >= kv * HEAD_DIM) & (lane < (kv + 1) * HEAD_DIM)
            s = _dot_nt(jnp.where(mask, qs, jnp.zeros_like(qs)), k_all)
            p = jnp.exp(s - jnp.max(s, axis=-1, keepdims=True))
            inv = 1.0 / jnp.sum(p, axis=-1, keepdims=True)
            o = _dot(p.astype(BF16), v_all)
            acc = acc + jnp.where(mask, o * inv, 0.0)
        o_ref[rows, g * KV_DIM:(g + 1) * KV_DIM] = acc.astype(BF16)


def _attn_prompt_kernel(q_ref, k_ref, v_ref, o_ref):
    for r in range(PROMPT_ATTN_SEQS):
        rows = pl.ds(r * SEQ, SEQ)
        _attn_heads(q_ref, rows, SEQ, k_ref[rows, :], v_ref[rows, :], o_ref)


def _attn_sample_kernel(q_ref, k_ref, v_ref, ck_ref, cv_ref, o_ref, kall_ref, vall_ref):
    @pl.when(pl.program_id(1) == 0)
    def _():
        kall_ref[:PAST_LEN] = ck_ref[0, 0].astype(BF16)
        kall_ref[PAST_LEN:] = k_ref[...]
        vall_ref[:PAST_LEN] = cv_ref[0, 0].astype(BF16)
        vall_ref[PAST_LEN:] = v_ref[...]

    _attn_heads(q_ref, pl.ds(0, ATTN_TQ), ATTN_TQ, kall_ref[...], vall_ref[...], o_ref)


def _attention(q, k, v, cache_k4, cache_v4, j):
    p_rows = PROMPT_ATTN_SEQS * SEQ
    o_p = pl.pallas_call(
        _attn_prompt_kernel,
        grid=(BATCH // PROMPT_ATTN_SEQS,),
        in_specs=[
            pl.BlockSpec((p_rows, D_MODEL), lambda b: (b, 0)),
            pl.BlockSpec((p_rows, KV_DIM), lambda b: (b, 0)),
            pl.BlockSpec((p_rows, KV_DIM), lambda b: (b, 0)),
        ],
        out_specs=pl.BlockSpec((p_rows, D_MODEL), lambda b: (b, 0)),
        out_shape=jax.ShapeDtypeStruct((N_PROMPT_TOK, D_MODEL), BF16),
        compiler_params=_params(("arbitrary",)),
        name="attn_prompt",
    )(q, k, v)
    q_blocks = DEC_SEQ // ATTN_TQ
    first_q = N_PROMPT_TOK // ATTN_TQ
    first_k = N_PROMPT_TOK // DEC_SEQ
    o_s = pl.pallas_call(
        _attn_sample_kernel,
        grid=(DEC_BATCH, q_blocks),
        in_specs=[
            pl.BlockSpec((ATTN_TQ, D_MODEL), lambda b, t: (first_q + b * q_blocks + t, 0)),
            pl.BlockSpec((DEC_SEQ, KV_DIM), lambda b, t: (first_k + b, 0)),
            pl.BlockSpec((DEC_SEQ, KV_DIM), lambda b, t: (first_k + b, 0)),
            pl.BlockSpec((1, 1, PAST_LEN, KV_DIM), lambda b, t: (b, j, 0, 0)),
            pl.BlockSpec((1, 1, PAST_LEN, KV_DIM), lambda b, t: (b, j, 0, 0)),
        ],
        out_specs=pl.BlockSpec((ATTN_TQ, D_MODEL), lambda b, t: (b * q_blocks + t, 0)),
        out_shape=jax.ShapeDtypeStruct((DEC_BATCH * DEC_SEQ, D_MODEL), BF16),
        scratch_shapes=[pltpu.VMEM((PAST_LEN + DEC_SEQ, KV_DIM), BF16),
                        pltpu.VMEM((PAST_LEN + DEC_SEQ, KV_DIM), BF16)],
        compiler_params=_params(("arbitrary", "arbitrary")),
        name="attn_sample",
    )(q, k, v, cache_k4, cache_v4)
    return o_p, o_s


def _oproj_kernel(*refs, n_x):
    x_refs = refs[:n_x]
    op_ref, os_ref = refs[n_x:n_x + 2]
    w_refs = refs[n_x + 2:n_x + 2 + N_GROUPS]
    m_ref, out_ref, wbf_ref = refs[n_x + 2 + N_GROUPS:]
    i = pl.program_id(0)

    @pl.when(i == 0)
    def _():
        for g in range(N_GROUPS):
            wbf_ref[g * KV_DIM:(g + 1) * KV_DIM, :] = w_refs[g][0, :, 0].reshape(KV_DIM, D_MODEL).astype(BF16)

    prompt = i < N_PROMPT_TILES
    gate = m_ref[0, 2, 0]
    for sb in range(TILE_M // PROJ_SUB):
        rows = pl.ds(sb * PROJ_SUB, PROJ_SUB)
        o = jnp.where(prompt, op_ref[rows, :], os_ref[rows, :])
        x = x_refs[0][rows, :] if n_x == 1 else jnp.where(prompt, x_refs[0][rows, :], x_refs[1][rows, :])
        out_ref[rows, :] = x + gate * _dot(o, wbf_ref[...])


def _oproj(xs, o_p, o_s, w_o5, j, mod):
    w_specs = [pl.BlockSpec((1, N_KV_HEADS, 1, HEAD_DIM, D_MODEL), lambda i, g=g: (j, 0, g, 0, 0))
               for g in range(N_GROUPS)]
    return pl.pallas_call(
        functools.partial(_oproj_kernel, n_x=len(xs)),
        grid=(N_TILES,),
        in_specs=_x_specs(len(xs), TILE_M, N_PROMPT_TILES) + [
            pl.BlockSpec((TILE_M, D_MODEL), lambda i: (jnp.minimum(i, N_PROMPT_TILES - 1), 0)),
            pl.BlockSpec((TILE_M, D_MODEL), lambda i: (jnp.maximum(i - N_PROMPT_TILES, 0), 0)),
        ] + w_specs + [_mod_spec(0, _tile_super)],
        out_specs=pl.BlockSpec((TILE_M, D_MODEL), lambda i: (i, 0)),
        out_shape=jax.ShapeDtypeStruct((N_TOK, D_MODEL), F32),
        scratch_shapes=[pltpu.VMEM((D_MODEL, D_MODEL), BF16)],
        compiler_params=_params(("arbitrary",)),
        name="attn_out_proj",
    )(*xs, o_p, o_s, *([w_o5] * N_GROUPS), mod)


def _conv_glu_kernel(x_ref, m_ref, g_ref, w_ref, b_ref, u_ref, wbf_ref):
    @pl.when(pl.program_id(0) == 0)
    def _():
        wbf_ref[...] = w_ref[0].astype(BF16)

    h = _norm_mod(x_ref[...], g_ref[...], m_ref[0, 0, 0], m_ref[0, 1, 0]).astype(BF16)
    z = _dot(h, wbf_ref[...]) + b_ref[0]
    u_ref[...] = z[:, :D_MODEL] * _sigmoid(z[:, D_MODEL:])


def _conv_glu(x, mod, g, w1, b1, j):
    return pl.pallas_call(
        _conv_glu_kernel,
        grid=(N_TILES,),
        in_specs=[
            pl.BlockSpec((TILE_M, D_MODEL), lambda i: (i, 0)),
            _mod_spec(0, _tile_super),
            pl.BlockSpec((1, D_MODEL), lambda i: (0, 0)),
            pl.BlockSpec((1, D_MODEL, 2 * D_MODEL), lambda i: (j, 0, 0)),
            pl.BlockSpec((1, 1, 2 * D_MODEL), lambda i: (j, 0, 0)),
        ],
        out_specs=pl.BlockSpec((TILE_M, D_MODEL), lambda i: (i, 0)),
        out_shape=jax.ShapeDtypeStruct((N_TOK, D_MODEL), F32),
        scratch_shapes=[pltpu.VMEM((D_MODEL, 2 * D_MODEL), BF16)],
        compiler_params=_params(("arbitrary",)),
        name="conv_glu",
    )(x, mod, g, w1, b1)


PAD_ROWS = CONV_HALO + SEQ_PER_SUPER * (SEQ + CONV_HALO)


def _conv_tail_kernel(x_ref, u_ref, wdw_ref, bdw_ref, lng_ref, lnb_ref, w2_ref, b2_ref, m_ref,
                      out_ref, pad_ref, cv_ref, wbf_ref):
    ss = pl.program_id(0)

    @pl.when(ss == 0)
    def _():
        wbf_ref[...] = w2_ref[0].astype(BF16)

    seq_gap = jnp.where(ss < N_PROMPT_SUPER, CONV_HALO, 0)
    pad_ref[...] = jnp.zeros_like(pad_ref)
    for s in range(SEQ_PER_SUPER):
        for j in range(D_MODEL // LANES):
            pad_ref[j, pl.ds(CONV_HALO + s * SEQ + s * seq_gap, SEQ), :] = (
                u_ref[pl.ds(s * SEQ, SEQ), j * LANES:(j + 1) * LANES])

    def block(c, carry):
        r0 = pl.multiple_of(c * CONV_ROWS, CONV_ROWS)
        base = r0 + (c // (SEQ // CONV_ROWS)) * seq_gap + (CONV_HALO - CONV_K // 2)
        for j in range(D_MODEL // LANES):
            cols = slice(j * LANES, (j + 1) * LANES)
            acc = jnp.zeros((CONV_ROWS, LANES), F32)
            for kk in range(CONV_K):
                acc = acc + wdw_ref[0, pl.ds(kk, 1), cols] * pad_ref[j, pl.ds(base + kk, CONV_ROWS), :]
            cv_ref[pl.ds(r0, CONV_ROWS), cols] = acc + bdw_ref[0, :, cols]
        return carry

    lax.fori_loop(0, SUPER // CONV_ROWS, block, 0)

    gate = m_ref[0, 2, 0]
    for sb in range(SUPER // CONV_TAIL_ROWS):
        rows = pl.ds(sb * CONV_TAIL_ROWS, CONV_TAIL_ROWS)
        cv = cv_ref[rows, :]
        mu = jnp.mean(cv, axis=-1, keepdims=True)
        cen = cv - mu
        var = jnp.mean(cen * cen, axis=-1, keepdims=True)
        y = cen * lax.rsqrt(var + EPS) * lng_ref[0] + lnb_ref[0]
        act = (y * _sigmoid(y)).astype(BF16)
        out_ref[rows, :] = x_ref[rows, :] + gate * (_dot(act, wbf_ref[...]) + b2_ref[0])


def _conv_tail(x, u, wdw, bdw, lng, lnb, w2, b2, mod, j):
    vec = pl.BlockSpec((1, 1, D_MODEL), lambda s: (j, 0, 0))
    return pl.pallas_call(
        _conv_tail_kernel,
        grid=(N_SUPER,),
        in_specs=[
            pl.BlockSpec((SUPER, D_MODEL), lambda s: (s, 0)),
            pl.BlockSpec((SUPER, D_MODEL), lambda s: (s, 0)),
            pl.BlockSpec((1, CONV_K, D_MODEL), lambda s: (j, 0, 0)),
            vec, vec, vec,
            pl.BlockSpec((1, D_MODEL, D_MODEL), lambda s: (j, 0, 0)),
            vec,
            _mod_spec(0, lambda s: s),
        ],
        out_specs=pl.BlockSpec((SUPER, D_MODEL), lambda s: (s, 0)),
        out_shape=jax.ShapeDtypeStruct((N_TOK, D_MODEL), F32),
        scratch_shapes=[pltpu.VMEM((D_MODEL // LANES, PAD_ROWS, LANES), F32),
                        pltpu.VMEM((SUPER, D_MODEL), F32),
                        pltpu.VMEM((D_MODEL, D_MODEL), BF16)],
        compiler_params=_params(("arbitrary",)),
        name="conv_tail",
    )(x, u, wdw, bdw, lng, lnb, w2, b2, mod)


def _route(aff, cap, tri):
    rows = aff.shape[0]
    capf = float(cap)

    def count_ge(bits):
        return jnp.sum((aff >= lax.bitcast_convert_type(bits, F32)).astype(F32), axis=1, keepdims=True)

    thr = jnp.zeros((rows, 1), I32)
    top = thr | (1 << 30)
    thr = jnp.where(count_ge(top) >= capf, top, thr)
    for shift in range(27, -1, -3):
        digit = jnp.zeros((rows, 1), I32)
        for d in range(1, 8):
            digit = digit + (count_ge(thr | (d << shift)) >= capf).astype(I32)
        thr = thr | (digit << shift)
    thr_f = lax.bitcast_convert_type(thr, F32)
    above = aff > thr_f
    tied = aff == thr_f
    need = capf - jnp.sum(above.astype(F32), axis=1, keepdims=True)
    tied_rank = _dot(tied.astype(BF16), tri)
    sel = above | (tied & (tied_rank <= need))
    return jnp.where(sel, _dot(sel.astype(BF16), tri) - 1.0, -1.0)


def _pad_rows_t(a, fill):
    pad = jnp.full((LANES - N_EXPERTS, a.shape[1]), fill, F32)
    return jnp.concatenate([a, pad], axis=0).T


def _moe_select_kernel(x_ref, m_ref, g_ref, wrt_ref, xs_ref, slot_ref, gate_ref, h_ref, tri_ref, p_ref):
    ss = pl.program_id(0)

    @pl.when(ss == 0)
    def _():
        r = lax.broadcasted_iota(I32, (SUPER, SUPER), 0)
        c = lax.broadcasted_iota(I32, (SUPER, SUPER), 1)
        tri_ref[...] = (r <= c).astype(BF16)

    h_ref[...] = _norm_mod(x_ref[...], g_ref[...], m_ref[0, 0, 0], m_ref[0, 1, 0]).astype(BF16)

    logit = _dot_nt(wrt_ref[0].astype(BF16), h_ref[...])
    ex = jnp.exp(logit - jnp.max(logit, axis=0, keepdims=True))
    aff = ex / jnp.sum(ex, axis=0, keepdims=True)

    @pl.when(ss < N_PROMPT_SUPER)
    def _():
        aff_r = jnp.concatenate([aff[:, b * SEQ:(b + 1) * SEQ] for b in range(SEQ_PER_SUPER)], axis=0)
        pos = _route(aff_r, SEQ_SLOTS, tri_ref[:SEQ, :SEQ])
        sub = lax.broadcasted_iota(I32, (SEQ_SLOTS, SEQ), 0).astype(F32)
        for b in range(SEQ_PER_SUPER):
            for e in range(N_EXPERTS):
                r = b * N_EXPERTS + e
                p_ref[e * SEQ_SLOTS:(e + 1) * SEQ_SLOTS, :] = (sub == pos[r:r + 1, :]).astype(BF16)
            rows = _dot(p_ref[...], h_ref[b * SEQ:(b + 1) * SEQ, :])
            xs_ref[:, b * SEQ_SLOTS:(b + 1) * SEQ_SLOTS, :] = (
                rows.astype(BF16).reshape(N_EXPERTS, SEQ_SLOTS, D_MODEL))
            slot_ref[b * SEQ:(b + 1) * SEQ, :] = _pad_rows_t(pos[b * N_EXPERTS:(b + 1) * N_EXPERTS], -1.0)
            gate_ref[b * SEQ:(b + 1) * SEQ, :] = _pad_rows_t(aff[:, b * SEQ:(b + 1) * SEQ], 0.0)

    @pl.when(ss >= N_PROMPT_SUPER)
    def _():
        pos = _route(aff, SLOTS, tri_ref[...])
        sub = lax.broadcasted_iota(I32, (SLOTS, SUPER), 0).astype(F32)
        for e0 in range(0, N_EXPERTS, GATHER_EXPERTS):
            onehot = jnp.concatenate([(sub == pos[e:e + 1, :]).astype(BF16)
                                      for e in range(e0, e0 + GATHER_EXPERTS)], axis=0)
            rows = _dot(onehot, h_ref[...])
            xs_ref[e0:e0 + GATHER_EXPERTS] = rows.astype(BF16).reshape(GATHER_EXPERTS, SLOTS, D_MODEL)
        slot_ref[...] = _pad_rows_t(pos, -1.0)
        gate_ref[...] = _pad_rows_t(aff, 0.0)


def _moe_select(x, mod, layer, g, wr_t):
    return pl.pallas_call(
        _moe_select_kernel,
        grid=(N_SUPER,),
        in_specs=[
            pl.BlockSpec((SUPER, D_MODEL), lambda s: (s, 0)),
            _mod_spec(1, lambda s: s),
            pl.BlockSpec((1, D_MODEL), lambda s: (0, 0)),
            pl.BlockSpec((1, N_EXPERTS, D_MODEL), lambda s: (layer, 0, 0)),
        ],
        out_specs=[
            pl.BlockSpec((N_EXPERTS, SLOTS, D_MODEL), lambda s: (0, s, 0)),
            pl.BlockSpec((SUPER, LANES), lambda s: (s, 0)),
            pl.BlockSpec((SUPER, LANES), lambda s: (s, 0)),
        ],
        out_shape=[
            jax.ShapeDtypeStruct((N_EXPERTS, N_SUPER * SLOTS, D_MODEL), BF16),
            jax.ShapeDtypeStruct((N_TOK, LANES), F32),
            jax.ShapeDtypeStruct((N_TOK, LANES), F32),
        ],
        scratch_shapes=[pltpu.VMEM((SUPER, D_MODEL), BF16),
                        pltpu.VMEM((SUPER, SUPER), BF16),
                        pltpu.VMEM((N_EXPERTS * SEQ_SLOTS, SEQ), BF16)],
        compiler_params=_params(("arbitrary",)),
        name="moe_select",
    )(x, mod, g, wr_t)


def _moe_ffn_kernel(x_ref, wg_ref, wu_ref, wd_ref, *refs, with_mod):
    if with_mod:
        cond_ref, wm_ref, bm_ref, y_ref, mod_ref, acc_ref = refs
        c = cond_ref[...]
        s = (c * _sigmoid(c)).astype(BF16)
        mod_ref[0] = _dot(s, wm_ref[0].astype(BF16)) + bm_ref[0, 0]
    else:
        y_ref, acc_ref = refs
    f = pl.program_id(1)
    x = x_ref[0]
    a = _dot(x, wg_ref[0, 0].astype(BF16))
    u = _dot(x, wu_ref[0, 0].astype(BF16))
    hid = (a * _sigmoid(a) * u).astype(BF16)
    acc_ref[f] = _dot(hid, wd_ref[0, 0].astype(BF16))

    @pl.when(f == N_FF_CHUNKS - 1)
    def _():
        y_ref[0] = (acc_ref[0] + acc_ref[1]).astype(BF16)


def _moe_ffn(xs, w_gate, w_up, w_down, layer, cond8=None, w_mod=None, b_mod=None):
    assert N_FF_CHUNKS == 2
    rows = N_SUPER * SLOTS
    with_mod = cond8 is not None
    in_specs = [
        pl.BlockSpec((1, rows, D_MODEL), lambda e, f: (e, 0, 0)),
        pl.BlockSpec((1, 1, D_MODEL, FF_CHUNK), lambda e, f: (layer, e, 0, f)),
        pl.BlockSpec((1, 1, D_MODEL, FF_CHUNK), lambda e, f: (layer, e, 0, f)),
        pl.BlockSpec((1, 1, FF_CHUNK, D_MODEL), lambda e, f: (layer, e, f, 0)),
    ]
    out_specs = [pl.BlockSpec((1, rows, D_MODEL), lambda e, f: (e, 0, 0))]
    out_shape = [jax.ShapeDtypeStruct((N_EXPERTS, rows, D_MODEL), BF16)]
    args = [xs, w_gate, w_up, w_down]
    if with_mod:
        n_chunks = N_MOD * D_MODEL // MOD_CHUNK
        assert n_chunks <= N_EXPERTS * N_FF_CHUNKS
        chunk = lambda e, f: jnp.minimum(e * N_FF_CHUNKS + f, n_chunks - 1)
        in_specs += [
            pl.BlockSpec((N_SUPER, D_MODEL), lambda e, f: (0, 0)),
            pl.BlockSpec((1, D_MODEL, MOD_CHUNK), lambda e, f: (layer + 1, 0, chunk(e, f))),
            pl.BlockSpec((1, 1, 1, MOD_CHUNK), lambda e, f: (layer + 1, chunk(e, f), 0, 0)),
        ]
        out_specs.append(pl.BlockSpec((1, N_SUPER, MOD_CHUNK), lambda e, f: (chunk(e, f), 0, 0)))
        out_shape.append(jax.ShapeDtypeStruct((n_chunks, N_SUPER, MOD_CHUNK), F32))
        args += [cond8, w_mod, b_mod.reshape(DEPTH, n_chunks, 1, MOD_CHUNK)]
    outs = pl.pallas_call(
        functools.partial(_moe_ffn_kernel, with_mod=with_mod),
        grid=(N_EXPERTS, N_FF_CHUNKS),
        in_specs=in_specs,
        out_specs=out_specs,
        out_shape=out_shape,
        scratch_shapes=[pltpu.VMEM((N_FF_CHUNKS, rows, D_MODEL), F32)],
        compiler_params=_params(("arbitrary", "arbitrary")),
        name="moe_ffn",
    )(*args)
    if not with_mod:
        return outs[0], None
    mod = outs[1].transpose(1, 0, 2).reshape(N_SUPER, N_MOD, D_MODEL).transpose(1, 0, 2)
    return outs[0], mod.reshape(2, 3, N_SUPER, 1, D_MODEL)


def _moe_combine_kernel(x_ref, y_ref, slot_ref, gate_ref, spread_ref, m_ref, *refs, final):
    out_refs, s_ref = refs[:-1], refs[-1]
    ss = pl.program_id(0)
    gate_f = m_ref[0, 2, 0]

    @pl.when(ss < N_PROMPT_SUPER)
    def _():
        lane_slot = (lax.broadcasted_iota(I32, (SEQ, LANES), 1) % SEQ_SLOTS).astype(F32)
        for b in range(SEQ_PER_SUPER):
            rows = pl.ds(b * SEQ, SEQ)
            slot_b = slot_ref[rows, :].astype(BF16)
            gate_b = gate_ref[rows, :].astype(BF16)
            for t in range(N_EXPERTS * SEQ_SLOTS // LANES):
                hit = _dot(slot_b, spread_ref[t]) == lane_slot
                s_ref[rows, t * LANES:(t + 1) * LANES] = jnp.where(hit, _dot(gate_b, spread_ref[t]), 0.0).astype(BF16)
            y_b = y_ref[:, b * SEQ_SLOTS:(b + 1) * SEQ_SLOTS, :].reshape(N_EXPERTS * SEQ_SLOTS, D_MODEL)
            res = x_ref[rows, :] + gate_f * _dot(s_ref[rows, :N_EXPERTS * SEQ_SLOTS], y_b)
            out_refs[0][rows, :] = res

    @pl.when(ss >= N_PROMPT_SUPER)
    def _():
        lane = lax.broadcasted_iota(I32, (SUPER, SLOTS), 1).astype(F32)
        moe = None
        for e0 in range(0, N_EXPERTS, SCATTER_EXPERTS):
            s_cols = jnp.concatenate(
                [jnp.where(slot_ref[:, e:e + 1] == lane, gate_ref[:, e:e + 1], 0.0).astype(BF16)
                 for e in range(e0, e0 + SCATTER_EXPERTS)], axis=1)
            part = _dot(s_cols, y_ref[e0:e0 + SCATTER_EXPERTS].reshape(SCATTER_EXPERTS * SLOTS, D_MODEL))
            moe = part if moe is None else moe + part
        out_refs[-1][...] = x_ref[...] + gate_f * moe


def _moe_combine(x, ys, slot_t, gate_t, spread, mod, final):
    if final:
        out_specs = [pl.BlockSpec((SUPER, D_MODEL), lambda s: (jnp.minimum(s, N_PROMPT_SUPER - 1), 0)),
                     pl.BlockSpec((SUPER, D_MODEL), lambda s: (jnp.maximum(s - N_PROMPT_SUPER, 0), 0))]
        out_shape = [jax.ShapeDtypeStruct((N_PROMPT_TOK, D_MODEL), F32),
                     jax.ShapeDtypeStruct((N_TOK - N_PROMPT_TOK, D_MODEL), F32)]
    else:
        out_specs = [pl.BlockSpec((SUPER, D_MODEL), lambda s: (s, 0))]
        out_shape = [jax.ShapeDtypeStruct((N_TOK, D_MODEL), F32)]
    return pl.pallas_call(
        functools.partial(_moe_combine_kernel, final=final),
        grid=(N_SUPER,),
        in_specs=[
            pl.BlockSpec((SUPER, D_MODEL), lambda s: (s, 0)),
            pl.BlockSpec((N_EXPERTS, SLOTS, D_MODEL), lambda s: (0, s, 0)),
            pl.BlockSpec((SUPER, LANES), lambda s: (s, 0)),
            pl.BlockSpec((SUPER, LANES), lambda s: (s, 0)),
            pl.BlockSpec((N_EXPERTS * SEQ_SLOTS // LANES, LANES, LANES), lambda s: (0, 0, 0)),
            _mod_spec(1, lambda s: s),
        ],
        out_specs=out_specs,
        out_shape=out_shape,
        scratch_shapes=[pltpu.VMEM((SUPER, N_EXPERTS * SEQ_SLOTS), BF16)],
        compiler_params=_params(("arbitrary",)),
        name="moe_combine",
    )(x, ys, slot_t, gate_t, spread, mod)


def _rope_tables():
    t = np.arange(DEC_SEQ)
    row = (t // GRID_W).astype(np.float32)
    col = (t % GRID_W).astype(np.float32)
    inv = (1.0 / (ROPE_THETA ** (np.arange(0, ROPE_AXIS_DIM, 2, dtype=np.float32) / ROPE_AXIS_DIM))).astype(np.float32)
    d = np.arange(LANES) % HEAD_DIM
    use_col = (d // ROPE_AXIS_DIM) == 1
    second_half = ((d % ROPE_AXIS_DIM) // (ROPE_AXIS_DIM // 2)) == 1
    ang = (np.where(use_col[None, :], col[:, None], row[:, None]) * inv[d % (ROPE_AXIS_DIM // 2)][None, :]).astype(np.float32)
    cos = np.cos(ang).astype(np.float32)
    sin = (np.where(second_half[None, :], 1.0, -1.0) * np.sin(ang)).astype(np.float32)
    return jnp.asarray(cos), jnp.asarray(sin)


def _head_block_ones():
    lane = np.arange(LANES)
    return jnp.asarray((lane[:, None] // HEAD_DIM) == (lane[None, :] // HEAD_DIM), dtype=BF16)


def _spread_tables():
    tiles = N_EXPERTS * SEQ_SLOTS // LANES
    e = np.arange(LANES)[None, :, None]
    lane = np.arange(LANES)[None, None, :]
    tile = np.arange(tiles)[:, None, None]
    return jnp.asarray(e == tile * (LANES // SEQ_SLOTS) + lane // SEQ_SLOTS, dtype=BF16)


def kernel(x_prompt, x_sample, cache_k, cache_v, c, c_ctx, w_mod, b_mod, norm_mix, norm_ffn, attn_w_qkv, attn_q_norm, attn_k_norm, attn_w_o, conv_w_pw1, conv_b_pw1, conv_w_dw, conv_b_dw, conv_ln_g, conv_ln_b, conv_w_pw2, conv_b_pw2, moe_w_router, moe_w_gate, moe_w_up, moe_w_down):
    xs = [x_prompt.reshape(N_PROMPT_TOK, D_MODEL), x_sample.reshape(N_TOK - N_PROMPT_TOK, D_MODEL)]
    cond8 = jnp.concatenate([jnp.broadcast_to(c_ctx[None, :], (N_PROMPT_SUPER, D_MODEL)), c], axis=0)
    mod = _adaln_first(cond8, w_mod, b_mod)

    cos_t, sin_t = _rope_tables()
    bd = _head_block_ones()
    spread = _spread_tables()
    wr_t = jnp.swapaxes(moe_w_router, 1, 2)
    cache_k4 = cache_k.reshape(DEC_BATCH, -1, PAST_LEN, KV_DIM)
    cache_v4 = cache_v.reshape(DEC_BATCH, -1, PAST_LEN, KV_DIM)
    w_o5 = attn_w_o.reshape(-1, N_KV_HEADS, N_GROUPS, HEAD_DIM, D_MODEL)

    new_k, new_v = [], []
    for i in range(DEPTH):
        j = i // N_MIXERS
        g_mix = norm_mix[i][None, :]
        g_ffn = norm_ffn[i][None, :]
        if i % N_MIXERS == 0:
            qg2 = jnp.tile(attn_q_norm[j], LANES // HEAD_DIM)[None, :]
            kg2 = jnp.tile(attn_k_norm[j], LANES // HEAD_DIM)[None, :]
            q, k, v, kc, vc = _qkv(xs, mod, g_mix, attn_w_qkv, j, qg2, kg2, cos_t, sin_t, bd)
            new_k.append(kc[:N_PROMPT_TOK].reshape(BATCH, SEQ, N_KV_HEADS, HEAD_DIM))
            new_v.append(vc[:N_PROMPT_TOK].reshape(BATCH, SEQ, N_KV_HEADS, HEAD_DIM))
            o_p, o_s = _attention(q, k, v, cache_k4, cache_v4, j)
            x = _oproj(xs, o_p, o_s, w_o5, j, mod)
        else:
            u = _conv_glu(xs[0], mod, g_mix, conv_w_pw1, conv_b_pw1[:, None, :], j)
            x = _conv_tail(xs[0], u, conv_w_dw, conv_b_dw[:, None, :], conv_ln_g[:, None, :], conv_ln_b[:, None, :],
                           conv_w_pw2, conv_b_pw2[:, None, :], mod, j)
        xg, slot_t, gate_t = _moe_select(x, mod, i, g_ffn, wr_t)
        last = i == DEPTH - 1
        ys, next_mod = _moe_ffn(xg, moe_w_gate, moe_w_up, moe_w_down, i,
                                *(() if last else (cond8, w_mod, b_mod)))
        xs = _moe_combine(x, ys, slot_t, gate_t, spread, mod, final=last)
        mod = next_mod

    y_prompt = xs[0].reshape(BATCH, SEQ, D_MODEL)
    y_sample = xs[1].reshape(DEC_BATCH, DEC_SEQ, D_MODEL)
    return (y_prompt, y_sample, jnp.stack(new_k, axis=1), jnp.stack(new_v, axis=1))
```

```python
import functools

import numpy as np

import jax
import jax.numpy as jnp
from jax import lax
from jax.experimental import pallas as pl
from jax.experimental.pallas import tpu as pltpu

F32 = jnp.float32
BF16 = jnp.bfloat16
I32 = jnp.int32

D_MODEL = 1024
BATCH = 16
SEQ = 256
DEPTH = 4
DEC_BATCH = 4
DEC_SEQ = 1024
PAST_LEN = 512
GRID_W = 64
N_MIXERS = 2
HEAD_DIM = 64
N_HEADS = 16
N_KV_HEADS = 4
N_GROUPS = N_HEADS // N_KV_HEADS
KV_DIM = N_KV_HEADS * HEAD_DIM
QKV_DIM = D_MODEL + 2 * KV_DIM
ROPE_AXIS_DIM = HEAD_DIM // 2
ROPE_THETA = 10000.0
CONV_K = 31
N_EXPERTS = 16
EXPERT_FF = 2 * D_MODEL
N_MOD = 6
EPS = 1e-6

N_PROMPT_TOK = BATCH * SEQ
N_TOK = N_PROMPT_TOK + DEC_BATCH * DEC_SEQ
SUPER = 1024
N_SUPER = N_TOK // SUPER
N_PROMPT_SUPER = N_PROMPT_TOK // SUPER
SEQ_PER_SUPER = SUPER // SEQ
SLOTS = SUPER // 8
SEQ_SLOTS = SEQ // 8
LANES = 128
TILE_M = 1024
N_TILES = N_TOK // TILE_M
N_PROMPT_TILES = N_PROMPT_TOK // TILE_M
QKV_SUB = 256
PROJ_SUB = 256
ATTN_TQ = 512
PROMPT_ATTN_SEQS = 4
MOD_COLS = 2 * D_MODEL
MOD_CHUNK = 256
FF_CHUNK = 1024
N_FF_CHUNKS = EXPERT_FF // FF_CHUNK
CONV_ROWS = 64
CONV_HALO = 16
CONV_TAIL_ROWS = 256
GATHER_EXPERTS = 4
SCATTER_EXPERTS = 2
VMEM_LIMIT = 56 * 1024 * 1024


def _dot(a, b):
    return jnp.dot(a, b, preferred_element_type=F32)


def _dot_nt(a, b):
    return lax.dot_general(a, b, (((1,), (1,)), ((), ())), preferred_element_type=F32)


def _sigmoid(x):
    return 1.0 / (1.0 + jnp.exp(-x))


def _norm_mod(x, g, shift, scale):
    y = x * lax.rsqrt(jnp.mean(x * x, axis=-1, keepdims=True) + EPS) * g
    return y * (1.0 + scale) + shift


def _params(semantics):
    return pltpu.CompilerParams(dimension_semantics=semantics, vmem_limit_bytes=VMEM_LIMIT)


def _mod_spec(which, super_of):
    return pl.BlockSpec((1, 3, 1, 1, D_MODEL), lambda *idx: (which, 0, super_of(*idx), 0, 0))


def _tile_super(i):
    return i // (SUPER // TILE_M)


def _x_specs(n_x, rows, n_prompt_blocks):
    if n_x == 1:
        return [pl.BlockSpec((rows, D_MODEL), lambda i: (i, 0))]
    return [pl.BlockSpec((rows, D_MODEL), lambda i: (jnp.minimum(i, n_prompt_blocks - 1), 0)),
            pl.BlockSpec((rows, D_MODEL), lambda i: (jnp.maximum(i - n_prompt_blocks, 0), 0))]


def _adaln_kernel(cond_ref, w_ref, b_ref, o_ref):
    c = cond_ref[...]
    s = (c * _sigmoid(c)).astype(BF16)
    res = _dot(s, w_ref[0].astype(BF16))
    for v in range(MOD_COLS // D_MODEL):
        o_ref[0, v] = res[:, v * D_MODEL:(v + 1) * D_MODEL] + b_ref[0, v]


def _adaln_first(cond8, w_mod, b_mod):
    b4 = b_mod.reshape(DEPTH, N_MOD, 1, D_MODEL)
    per_step = MOD_COLS // D_MODEL
    out = pl.pallas_call(
        _adaln_kernel,
        grid=(N_MOD // per_step,),
        in_specs=[
            pl.BlockSpec((N_SUPER, D_MODEL), lambda n: (0, 0)),
            pl.BlockSpec((1, D_MODEL, MOD_COLS), lambda n: (0, 0, n)),
            pl.BlockSpec((1, per_step, 1, D_MODEL), lambda n: (0, n, 0, 0)),
        ],
        out_specs=pl.BlockSpec((1, per_step, N_SUPER, D_MODEL), lambda n: (0, n, 0, 0)),
        out_shape=jax.ShapeDtypeStruct((1, N_MOD, N_SUPER, D_MODEL), F32),
        compiler_params=_params(("arbitrary",)),
        name="adaln",
    )(cond8, w_mod, b4)
    return out.reshape(2, 3, N_SUPER, 1, D_MODEL)


def _qkv_kernel(*refs, n_x):
    x_refs = refs[:n_x]
    (m_ref, g_ref, w_ref, qg_ref, kg_ref, cos_ref, sin_ref, bd_ref,
     q_ref, k_ref, v_ref, kc_ref, vc_ref, wbf_ref) = refs[n_x:]
    i = pl.program_id(0)

    @pl.when(i == 0)
    def _():
        wbf_ref[...] = w_ref[0].astype(BF16)

    bd = bd_ref[...]
    lane = lax.broadcasted_iota(I32, (QKV_SUB, LANES), 1)
    first_half = (lane & (ROPE_AXIS_DIM // 2)) == 0
    low_head = lane < HEAD_DIM
    latent = i >= N_PROMPT_SUPER
    qg = qg_ref[...] * (HEAD_DIM ** -0.5)
    kg = kg_ref[...]
    g_mix = g_ref[...]
    shift = m_ref[0, 0, 0]
    scale = m_ref[0, 1, 0]

    def head_norm(t, gain):
        sq = t * t
        hi = sq.astype(BF16)
        lo = (sq - hi.astype(F32)).astype(BF16)
        ssum = _dot(hi, bd) + _dot(lo, bd)
        return t * lax.rsqrt(ssum * (1.0 / HEAD_DIM) + EPS) * gain

    for sb in range(SUPER // QKV_SUB):
        rows = pl.ds(sb * QKV_SUB, QKV_SUB)
        x = x_refs[0][rows, :] if n_x == 1 else jnp.where(latent, x_refs[1][rows, :], x_refs[0][rows, :])
        h = _norm_mod(x, g_mix, shift, scale).astype(BF16)
        qkv = _dot(h, wbf_ref[...])
        cos = cos_ref[rows, :]
        sin = sin_ref[rows, :]

        def rope(y):
            partner = jnp.where(first_half, pltpu.roll(y, LANES - ROPE_AXIS_DIM // 2, 1),
                                pltpu.roll(y, ROPE_AXIS_DIM // 2, 1))
            return jnp.where(latent, y * cos + partner * sin, y)

        q_tiles = [rope(head_norm(qkv[:, j * LANES:(j + 1) * LANES], qg)) for j in range(D_MODEL // LANES)]
        for t in range(D_MODEL // LANES):
            grp = t // 2
            src = (4 * (2 * (t % 2)) + grp) // 2
            lo_part = q_tiles[src] if grp % 2 == 0 else pltpu.roll(q_tiles[src], HEAD_DIM, 1)
            hi_part = q_tiles[src + 2] if grp % 2 == 1 else pltpu.roll(q_tiles[src + 2], HEAD_DIM, 1)
            q_ref[rows, t * LANES:(t + 1) * LANES] = jnp.where(low_head, lo_part, hi_part).astype(BF16)
        for j in range(KV_DIM // LANES):
            y = head_norm(qkv[:, D_MODEL + j * LANES:D_MODEL + (j + 1) * LANES], kg)
            k_ref[rows, j * LANES:(j + 1) * LANES] = rope(y).astype(BF16)
            kc_ref[rows, j * LANES:(j + 1) * LANES] = y
        v = qkv[:, D_MODEL + KV_DIM:]
        v_ref[rows, :] = v.astype(BF16)
        vc_ref[rows, :] = v


def _qkv(xs, mod, g, w_qkv, j, qg2, kg2, cos_t, sin_t, bd):
    tile = lambda i: (i, 0)
    const = lambda i: (0, 0)
    return pl.pallas_call(
        functools.partial(_qkv_kernel, n_x=len(xs)),
        grid=(N_SUPER,),
        in_specs=_x_specs(len(xs), SUPER, N_PROMPT_SUPER) + [
            _mod_spec(0, lambda i: i),
            pl.BlockSpec((1, D_MODEL), const),
            pl.BlockSpec((1, D_MODEL, QKV_DIM), lambda i: (j, 0, 0)),
            pl.BlockSpec((1, LANES), const),
            pl.BlockSpec((1, LANES), const),
            pl.BlockSpec((DEC_SEQ, LANES), const),
            pl.BlockSpec((DEC_SEQ, LANES), const),
            pl.BlockSpec((LANES, LANES), const),
        ],
        out_specs=[
            pl.BlockSpec((SUPER, D_MODEL), tile),
            pl.BlockSpec((SUPER, KV_DIM), tile),
            pl.BlockSpec((SUPER, KV_DIM), tile),
            pl.BlockSpec((SUPER, KV_DIM), tile),
            pl.BlockSpec((SUPER, KV_DIM), tile),
        ],
        out_shape=[
            jax.ShapeDtypeStruct((N_TOK, D_MODEL), BF16),
            jax.ShapeDtypeStruct((N_TOK, KV_DIM), BF16),
            jax.ShapeDtypeStruct((N_TOK, KV_DIM), BF16),
            jax.ShapeDtypeStruct((N_TOK, KV_DIM), F32),
            jax.ShapeDtypeStruct((N_TOK, KV_DIM), F32),
        ],
        scratch_shapes=[pltpu.VMEM((D_MODEL, QKV_DIM), BF16)],
        compiler_params=_params(("arbitrary",)),
        name="qkv_proj",
    )(*xs, mod, g, w_qkv, qg2, kg2, cos_t, sin_t, bd)


def _attn_heads(q_ref, rows, n_rows, k_all, v_all, o_ref):
    lane = lax.broadcasted_iota(I32, (n_rows, KV_DIM), 1)
    for g in range(N_GROUPS):
        qs = q_ref[rows, g * KV_DIM:(g + 1) * KV_DIM]
        acc = jnp.zeros((n_rows, KV_DIM), F32)
        for kv in range(N_KV_HEADS):
            mask = (lane >= kv * HEAD_DIM) & (lane < (kv + 1) * HEAD_DIM)
            s = _dot_nt(jnp.where(mask, qs, jnp.zeros_like(qs)), k_all)
            p = jnp.exp(s - jnp.max(s, axis=-1, keepdims=True))
            inv = 1.0 / jnp.sum(p, axis=-1, keepdims=True)
            o = _dot(p.astype(BF16), v_all)
            acc = acc + jnp.where(mask, o * inv, 0.0)
        o_ref[rows, g * KV_DIM:(g + 1) * KV_DIM] = acc.astype(BF16)


def _attn_prompt_kernel(q_ref, k_ref, v_ref, o_ref):
    for r in range(PROMPT_ATTN_SEQS):
        rows = pl.ds(r * SEQ, SEQ)
        _attn_heads(q_ref, rows, SEQ, k_ref[rows, :], v_ref[rows, :], o_ref)


def _attn_sample_kernel(q_ref, k_ref, v_ref, ck_ref, cv_ref, o_ref, kall_ref, vall_ref):
    @pl.when(pl.program_id(1) == 0)
    def _():
        kall_ref[:PAST_LEN] = ck_ref[0, 0].astype(BF16)
        kall_ref[PAST_LEN:] = k_ref[...]
        vall_ref[:PAST_LEN] = cv_ref[0, 0].astype(BF16)
        vall_ref[PAST_LEN:] = v_ref[...]

    _attn_heads(q_ref, pl.ds(0, ATTN_TQ), ATTN_TQ, kall_ref[...], vall_ref[...], o_ref)


def _attention(q, k, v, cache_k4, cache_v4, j):
    p_rows = PROMPT_ATTN_SEQS * SEQ
    o_p = pl.pallas_call(
        _attn_prompt_kernel,
        grid=(BATCH // PROMPT_ATTN_SEQS,),
        in_specs=[
            pl.BlockSpec((p_rows, D_MODEL), lambda b: (b, 0)),
            pl.BlockSpec((p_rows, KV_DIM), lambda b: (b, 0)),
            pl.BlockSpec((p_rows, KV_DIM), lambda b: (b, 0)),
        ],
        out_specs=pl.BlockSpec((p_rows, D_MODEL), lambda b: (b, 0)),
        out_shape=jax.ShapeDtypeStruct((N_PROMPT_TOK, D_MODEL), BF16),
        compiler_params=_params(("arbitrary",)),
        name="attn_prompt",
    )(q, k, v)
    q_blocks = DEC_SEQ // ATTN_TQ
    first_q = N_PROMPT_TOK // ATTN_TQ
    first_k = N_PROMPT_TOK // DEC_SEQ
    o_s = pl.pallas_call(
        _attn_sample_kernel,
        grid=(DEC_BATCH, q_blocks),
        in_specs=[
            pl.BlockSpec((ATTN_TQ, D_MODEL), lambda b, t: (first_q + b * q_blocks + t, 0)),
            pl.BlockSpec((DEC_SEQ, KV_DIM), lambda b, t: (first_k + b, 0)),
            pl.BlockSpec((DEC_SEQ, KV_DIM), lambda b, t: (first_k + b, 0)),
            pl.BlockSpec((1, 1, PAST_LEN, KV_DIM), lambda b, t: (b, j, 0, 0)),
            pl.BlockSpec((1, 1, PAST_LEN, KV_DIM), lambda b, t: (b, j, 0, 0)),
        ],
        out_specs=pl.BlockSpec((ATTN_TQ, D_MODEL), lambda b, t: (b * q_blocks + t, 0)),
        out_shape=jax.ShapeDtypeStruct((DEC_BATCH * DEC_SEQ, D_MODEL), BF16),
        scratch_shapes=[pltpu.VMEM((PAST_LEN + DEC_SEQ, KV_DIM), BF16),
                        pltpu.VMEM((PAST_LEN + DEC_SEQ, KV_DIM), BF16)],
        compiler_params=_params(("arbitrary", "arbitrary")),
        name="attn_sample",
    )(q, k, v, cache_k4, cache_v4)
    return o_p, o_s


def _oproj_kernel(*refs, n_x):
    x_refs = refs[:n_x]
    op_ref, os_ref = refs[n_x:n_x + 2]
    w_refs = refs[n_x + 2:n_x + 2 + N_GROUPS]
    m_ref, out_ref, wbf_ref = refs[n_x + 2 + N_GROUPS:]
    i = pl.program_id(0)

    @pl.when(i == 0)
    def _():
        for g in range(N_GROUPS):
            wbf_ref[g * KV_DIM:(g + 1) * KV_DIM, :] = w_refs[g][0, :, 0].reshape(KV_DIM, D_MODEL).astype(BF16)

    prompt = i < N_PROMPT_TILES
    gate = m_ref[0, 2, 0]
    for sb in range(TILE_M // PROJ_SUB):
        rows = pl.ds(sb * PROJ_SUB, PROJ_SUB)
        o = jnp.where(prompt, op_ref[rows, :], os_ref[rows, :])
        x = x_refs[0][rows, :] if n_x == 1 else jnp.where(prompt, x_refs[0][rows, :], x_refs[1][rows, :])
        out_ref[rows, :] = x + gate * _dot(o, wbf_ref[...])


def _oproj(xs, o_p, o_s, w_o5, j, mod):
    w_specs = [pl.BlockSpec((1, N_KV_HEADS, 1, HEAD_DIM, D_MODEL), lambda i, g=g: (j, 0, g, 0, 0))
               for g in range(N_GROUPS)]
    return pl.pallas_call(
        functools.partial(_oproj_kernel, n_x=len(xs)),
        grid=(N_TILES,),
        in_specs=_x_specs(len(xs), TILE_M, N_PROMPT_TILES) + [
            pl.BlockSpec((TILE_M, D_MODEL), lambda i: (jnp.minimum(i, N_PROMPT_TILES - 1), 0)),
            pl.BlockSpec((TILE_M, D_MODEL), lambda i: (jnp.maximum(i - N_PROMPT_TILES, 0), 0)),
        ] + w_specs + [_mod_spec(0, _tile_super)],
        out_specs=pl.BlockSpec((TILE_M, D_MODEL), lambda i: (i, 0)),
        out_shape=jax.ShapeDtypeStruct((N_TOK, D_MODEL), F32),
        scratch_shapes=[pltpu.VMEM((D_MODEL, D_MODEL), BF16)],
        compiler_params=_params(("arbitrary",)),
        name="attn_out_proj",
    )(*xs, o_p, o_s, *([w_o5] * N_GROUPS), mod)


PAD_ROWS = CONV_HALO + SEQ_PER_SUPER * (SEQ + CONV_HALO)


def _conv_kernel(x_ref, g_ref, w1_ref, b1_ref, wdw_ref, bdw_ref, lng_ref, lnb_ref, w2_ref, b2_ref, m_ref,
                 out_ref, pad_ref, cv_ref, w1bf_ref, wbf_ref):
    ss = pl.program_id(0)

    @pl.when(ss == 0)
    def _():
        w1bf_ref[...] = w1_ref[0].astype(BF16)
        wbf_ref[...] = w2_ref[0].astype(BF16)

    seq_gap = jnp.where(ss < N_PROMPT_SUPER, CONV_HALO, 0)
    pad_ref[...] = jnp.zeros_like(pad_ref)
    g_mix = g_ref[...]
    shift = m_ref[0, 0, 0]
    scale = m_ref[0, 1, 0]
    for s in range(SEQ_PER_SUPER):
        h = _norm_mod(x_ref[pl.ds(s * SEQ, SEQ), :], g_mix, shift, scale).astype(BF16)
        z = _dot(h, w1bf_ref[...]) + b1_ref[0]
        u = z[:, :D_MODEL] * _sigmoid(z[:, D_MODEL:])
        for j in range(D_MODEL // LANES):
            pad_ref[j, pl.ds(CONV_HALO + s * SEQ + s * seq_gap, SEQ), :] = u[:, j * LANES:(j + 1) * LANES]

    def block(c, carry):
        r0 = pl.multiple_of(c * CONV_ROWS, CONV_ROWS)
        base = r0 + (c // (SEQ // CONV_ROWS)) * seq_gap + (CONV_HALO - CONV_K // 2)
        for j in range(D_MODEL // LANES):
            cols = slice(j * LANES, (j + 1) * LANES)
            acc = jnp.zeros((CONV_ROWS, LANES), F32)
            for kk in range(CONV_K):
                acc = acc + wdw_ref[0, pl.ds(kk, 1), cols] * pad_ref[j, pl.ds(base + kk, CONV_ROWS), :]
            cv_ref[pl.ds(r0, CONV_ROWS), cols] = acc + bdw_ref[0, :, cols]
        return carry

    lax.fori_loop(0, SUPER // CONV_ROWS, block, 0)

    gate = m_ref[0, 2, 0]
    for sb in range(SUPER // CONV_TAIL_ROWS):
        rows = pl.ds(sb * CONV_TAIL_ROWS, CONV_TAIL_ROWS)
        cv = cv_ref[rows, :]
        mu = jnp.mean(cv, axis=-1, keepdims=True)
        cen = cv - mu
        var = jnp.mean(cen * cen, axis=-1, keepdims=True)
        y = cen * lax.rsqrt(var + EPS) * lng_ref[0] + lnb_ref[0]
        act = (y * _sigmoid(y)).astype(BF16)
        out_ref[rows, :] = x_ref[rows, :] + gate * (_dot(act, wbf_ref[...]) + b2_ref[0])


def _conv(x, g, w1, b1, wdw, bdw, lng, lnb, w2, b2, mod, j):
    vec = pl.BlockSpec((1, 1, D_MODEL), lambda s: (j, 0, 0))
    return pl.pallas_call(
        _conv_kernel,
        grid=(N_SUPER,),
        in_specs=[
            pl.BlockSpec((SUPER, D_MODEL), lambda s: (s, 0)),
            pl.BlockSpec((1, D_MODEL), lambda s: (0, 0)),
            pl.BlockSpec((1, D_MODEL, 2 * D_MODEL), lambda s: (j, 0, 0)),
            pl.BlockSpec((1, 1, 2 * D_MODEL), lambda s: (j, 0, 0)),
            pl.BlockSpec((1, CONV_K, D_MODEL), lambda s: (j, 0, 0)),
            vec, vec, vec,
            pl.BlockSpec((1, D_MODEL, D_MODEL), lambda s: (j, 0, 0)),
            vec,
            _mod_spec(0, lambda s: s),
        ],
        out_specs=pl.BlockSpec((SUPER, D_MODEL), lambda s: (s, 0)),
        out_shape=jax.ShapeDtypeStruct((N_TOK, D_MODEL), F32),
        scratch_shapes=[pltpu.VMEM((D_MODEL // LANES, PAD_ROWS, LANES), F32),
                        pltpu.VMEM((SUPER, D_MODEL), F32),
                        pltpu.VMEM((D_MODEL, 2 * D_MODEL), BF16),
                        pltpu.VMEM((D_MODEL, D_MODEL), BF16)],
        compiler_params=_params(("arbitrary",)),
        name="conv_module",
    )(x, g, w1, b1, wdw, bdw, lng, lnb, w2, b2, mod)


def _route(aff, cap, tri):
    rows = aff.shape[0]
    capf = float(cap)

    def count_ge(bits):
        return jnp.sum((aff >= lax.bitcast_convert_type(bits, F32)).astype(F32), axis=1, keepdims=True)

    thr = jnp.zeros((rows, 1), I32)
    top = thr | (1 << 30)
    thr = jnp.where(count_ge(top) >= capf, top, thr)
    for shift in range(27, -1, -3):
        digit = jnp.zeros((rows, 1), I32)
        for d in range(1, 8):
            digit = digit + (count_ge(thr | (d << shift)) >= capf).astype(I32)
        thr = thr | (digit << shift)
    thr_f = lax.bitcast_convert_type(thr, F32)
    above = aff > thr_f
    tied = aff == thr_f
    need = capf - jnp.sum(above.astype(F32), axis=1, keepdims=True)
    tied_rank = _dot(tied.astype(BF16), tri)
    sel = above | (tied & (tied_rank <= need))
    return jnp.where(sel, _dot(sel.astype(BF16), tri) - 1.0, -1.0)


def _pad_rows_t(a, fill):
    pad = jnp.full((LANES - N_EXPERTS, a.shape[1]), fill, F32)
    return jnp.concatenate([a, pad], axis=0).T


def _moe_select_kernel(x_ref, m_ref, g_ref, wrt_ref, xs_ref, slot_ref, gate_ref, h_ref, tri_ref, p_ref):
    ss = pl.program_id(0)

    @pl.when(ss == 0)
    def _():
        r = lax.broadcasted_iota(I32, (SUPER, SUPER), 0)
        c = lax.broadcasted_iota(I32, (SUPER, SUPER), 1)
        tri_ref[...] = (r <= c).astype(BF16)

    h_ref[...] = _norm_mod(x_ref[...], g_ref[...], m_ref[0, 0, 0], m_ref[0, 1, 0]).astype(BF16)

    logit = _dot_nt(wrt_ref[0].astype(BF16), h_ref[...])
    ex = jnp.exp(logit - jnp.max(logit, axis=0, keepdims=True))
    aff = ex / jnp.sum(ex, axis=0, keepdims=True)

    @pl.when(ss < N_PROMPT_SUPER)
    def _():
        aff_r = jnp.concatenate([aff[:, b * SEQ:(b + 1) * SEQ] for b in range(SEQ_PER_SUPER)], axis=0)
        pos = _route(aff_r, SEQ_SLOTS, tri_ref[:SEQ, :SEQ])
        sub = lax.broadcasted_iota(I32, (SEQ_SLOTS, SEQ), 0).astype(F32)
        for b in range(SEQ_PER_SUPER):
            for e in range(N_EXPERTS):
                r = b * N_EXPERTS + e
                p_ref[e * SEQ_SLOTS:(e + 1) * SEQ_SLOTS, :] = (sub == pos[r:r + 1, :]).astype(BF16)
            rows = _dot(p_ref[...], h_ref[b * SEQ:(b + 1) * SEQ, :])
            xs_ref[:, b * SEQ_SLOTS:(b + 1) * SEQ_SLOTS, :] = (
                rows.astype(BF16).reshape(N_EXPERTS, SEQ_SLOTS, D_MODEL))
            slot_ref[b * SEQ:(b + 1) * SEQ, :] = _pad_rows_t(pos[b * N_EXPERTS:(b + 1) * N_EXPERTS], -1.0)
            gate_ref[b * SEQ:(b + 1) * SEQ, :] = _pad_rows_t(aff[:, b * SEQ:(b + 1) * SEQ], 0.0)

    @pl.when(ss >= N_PROMPT_SUPER)
    def _():
        pos = _route(aff, SLOTS, tri_ref[...])
        sub = lax.broadcasted_iota(I32, (SLOTS, SUPER), 0).astype(F32)
        for e0 in range(0, N_EXPERTS, GATHER_EXPERTS):
            onehot = jnp.concatenate([(sub == pos[e:e + 1, :]).astype(BF16)
                                      for e in range(e0, e0 + GATHER_EXPERTS)], axis=0)
            rows = _dot(onehot, h_ref[...])
            xs_ref[e0:e0 + GATHER_EXPERTS] = rows.astype(BF16).reshape(GATHER_EXPERTS, SLOTS, D_MODEL)
        slot_ref[...] = _pad_rows_t(pos, -1.0)
        gate_ref[...] = _pad_rows_t(aff, 0.0)


def _moe_select(x, mod, layer, g, wr_t):
    return pl.pallas_call(
        _moe_select_kernel,
        grid=(N_SUPER,),
        in_specs=[
            pl.BlockSpec((SUPER, D_MODEL), lambda s: (s, 0)),
            _mod_spec(1, lambda s: s),
            pl.BlockSpec((1, D_MODEL), lambda s: (0, 0)),
            pl.BlockSpec((1, N_EXPERTS, D_MODEL), lambda s: (layer, 0, 0)),
        ],
        out_specs=[
            pl.BlockSpec((N_EXPERTS, SLOTS, D_MODEL), lambda s: (0, s, 0)),
            pl.BlockSpec((SUPER, LANES), lambda s: (s, 0)),
            pl.BlockSpec((SUPER, LANES), lambda s: (s, 0)),
        ],
        out_shape=[
            jax.ShapeDtypeStruct((N_EXPERTS, N_SUPER * SLOTS, D_MODEL), BF16),
            jax.ShapeDtypeStruct((N_TOK, LANES), F32),
            jax.ShapeDtypeStruct((N_TOK, LANES), F32),
        ],
        scratch_shapes=[pltpu.VMEM((SUPER, D_MODEL), BF16),
                        pltpu.VMEM((SUPER, SUPER), BF16),
                        pltpu.VMEM((N_EXPERTS * SEQ_SLOTS, SEQ), BF16)],
        compiler_params=_params(("arbitrary",)),
        name="moe_select",
    )(x, mod, g, wr_t)


def _moe_ffn_kernel(x_ref, wg_ref, wu_ref, wd_ref, *refs, with_mod):
    if with_mod:
        cond_ref, wm_ref, bm_ref, y_ref, mod_ref, acc_ref = refs
        c = cond_ref[...]
        s = (c * _sigmoid(c)).astype(BF16)
        mod_ref[0] = _dot(s, wm_ref[0].astype(BF16)) + bm_ref[0, 0]
    else:
        y_ref, acc_ref = refs
    f = pl.program_id(1)
    x = x_ref[0]
    a = _dot(x, wg_ref[0, 0].astype(BF16))
    u = _dot(x, wu_ref[0, 0].astype(BF16))
    hid = (a * _sigmoid(a) * u).astype(BF16)
    acc_ref[f] = _dot(hid, wd_ref[0, 0].astype(BF16))

    @pl.when(f == N_FF_CHUNKS - 1)
    def _():
        y_ref[0] = (acc_ref[0] + acc_ref[1]).astype(BF16)


def _moe_ffn(xs, w_gate, w_up, w_down, layer, cond8=None, w_mod=None, b_mod=None):
    assert N_FF_CHUNKS == 2
    rows = N_SUPER * SLOTS
    with_mod = cond8 is not None
    in_specs = [
        pl.BlockSpec((1, rows, D_MODEL), lambda e, f: (e, 0, 0)),
        pl.BlockSpec((1, 1, D_MODEL, FF_CHUNK), lambda e, f: (layer, e, 0, f)),
        pl.BlockSpec((1, 1, D_MODEL, FF_CHUNK), lambda e, f: (layer, e, 0, f)),
        pl.BlockSpec((1, 1, FF_CHUNK, D_MODEL), lambda e, f: (layer, e, f, 0)),
    ]
    out_specs = [pl.BlockSpec((1, rows, D_MODEL), lambda e, f: (e, 0, 0))]
    out_shape = [jax.ShapeDtypeStruct((N_EXPERTS, rows, D_MODEL), BF16)]
    args = [xs, w_gate, w_up, w_down]
    if with_mod:
        n_chunks = N_MOD * D_MODEL // MOD_CHUNK
        assert n_chunks <= N_EXPERTS * N_FF_CHUNKS
        chunk = lambda e, f: jnp.minimum(e * N_FF_CHUNKS + f, n_chunks - 1)
        in_specs += [
            pl.BlockSpec((N_SUPER, D_MODEL), lambda e, f: (0, 0)),
            pl.BlockSpec((1, D_MODEL, MOD_CHUNK), lambda e, f: (layer + 1, 0, chunk(e, f))),
            pl.BlockSpec((1, 1, 1, MOD_CHUNK), lambda e, f: (layer + 1, chunk(e, f), 0, 0)),
        ]
        out_specs.append(pl.BlockSpec((1, N_SUPER, MOD_CHUNK), lambda e, f: (chunk(e, f), 0, 0)))
        out_shape.append(jax.ShapeDtypeStruct((n_chunks, N_SUPER, MOD_CHUNK), F32))
        args += [cond8, w_mod, b_mod.reshape(DEPTH, n_chunks, 1, MOD_CHUNK)]
    outs = pl.pallas_call(
        functools.partial(_moe_ffn_kernel, with_mod=with_mod),
        grid=(N_EXPERTS, N_FF_CHUNKS),
        in_specs=in_specs,
        out_specs=out_specs,
        out_shape=out_shape,
        scratch_shapes=[pltpu.VMEM((N_FF_CHUNKS, rows, D_MODEL), F32)],
        compiler_params=_params(("arbitrary", "arbitrary")),
        name="moe_ffn",
    )(*args)
    if not with_mod:
        return outs[0], None
    mod = outs[1].transpose(1, 0, 2).reshape(N_SUPER, N_MOD, D_MODEL).transpose(1, 0, 2)
    return outs[0], mod.reshape(2, 3, N_SUPER, 1, D_MODEL)


def _moe_combine_kernel(x_ref, y_ref, slot_ref, gate_ref, spread_ref, m_ref, *refs, final):
    out_refs, s_ref = refs[:-1], refs[-1]
    ss = pl.program_id(0)
    gate_f = m_ref[0, 2, 0]

    @pl.when(ss < N_PROMPT_SUPER)
    def _():
        lane_slot = (lax.broadcasted_iota(I32, (SEQ, LANES), 1) % SEQ_SLOTS).astype(F32)
        for b in range(SEQ_PER_SUPER):
            rows = pl.ds(b * SEQ, SEQ)
            slot_b = slot_ref[rows, :].astype(BF16)
            gate_b = gate_ref[rows, :].astype(BF16)
            for t in range(N_EXPERTS * SEQ_SLOTS // LANES):
                hit = _dot(slot_b, spread_ref[t]) == lane_slot
                s_ref[rows, t * LANES:(t + 1) * LANES] = jnp.where(hit, _dot(gate_b, spread_ref[t]), 0.0).astype(BF16)
            y_b = y_ref[:, b * SEQ_SLOTS:(b + 1) * SEQ_SLOTS, :].reshape(N_EXPERTS * SEQ_SLOTS, D_MODEL)
            res = x_ref[rows, :] + gate_f * _dot(s_ref[rows, :N_EXPERTS * SEQ_SLOTS], y_b)
            out_refs[0][rows, :] = res

    @pl.when(ss >= N_PROMPT_SUPER)
    def _():
        lane = lax.broadcasted_iota(I32, (SUPER, SLOTS), 1).astype(F32)
        moe = None
        for e0 in range(0, N_EXPERTS, SCATTER_EXPERTS):
            s_cols = jnp.concatenate(
                [jnp.where(slot_ref[:, e:e + 1] == lane, gate_ref[:, e:e + 1], 0.0).astype(BF16)
                 for e in range(e0, e0 + SCATTER_EXPERTS)], axis=1)
            part = _dot(s_cols, y_ref[e0:e0 + SCATTER_EXPERTS].reshape(SCATTER_EXPERTS * SLOTS, D_MODEL))
            moe = part if moe is None else moe + part
        out_refs[-1][...] = x_ref[...] + gate_f * moe


def _moe_combine(x, ys, slot_t, gate_t, spread, mod, final):
    if final:
        out_specs = [pl.BlockSpec((SUPER, D_MODEL), lambda s: (jnp.minimum(s, N_PROMPT_SUPER - 1), 0)),
                     pl.BlockSpec((SUPER, D_MODEL), lambda s: (jnp.maximum(s - N_PROMPT_SUPER, 0), 0))]
        out_shape = [jax.ShapeDtypeStruct((N_PROMPT_TOK, D_MODEL), F32),
                     jax.ShapeDtypeStruct((N_TOK - N_PROMPT_TOK, D_MODEL), F32)]
    else:
        out_specs = [pl.BlockSpec((SUPER, D_MODEL), lambda s: (s, 0))]
        out_shape = [jax.ShapeDtypeStruct((N_TOK, D_MODEL), F32)]
    return pl.pallas_call(
        functools.partial(_moe_combine_kernel, final=final),
        grid=(N_SUPER,),
        in_specs=[
            pl.BlockSpec((SUPER, D_MODEL), lambda s: (s, 0)),
            pl.BlockSpec((N_EXPERTS, SLOTS, D_MODEL), lambda s: (0, s, 0)),
            pl.BlockSpec((SUPER, LANES), lambda s: (s, 0)),
            pl.BlockSpec((SUPER, LANES), lambda s: (s, 0)),
            pl.BlockSpec((N_EXPERTS * SEQ_SLOTS // LANES, LANES, LANES), lambda s: (0, 0, 0)),
            _mod_spec(1, lambda s: s),
        ],
        out_specs=out_specs,
        out_shape=out_shape,
        scratch_shapes=[pltpu.VMEM((SUPER, N_EXPERTS * SEQ_SLOTS), BF16)],
        compiler_params=_params(("arbitrary",)),
        name="moe_combine",
    )(x, ys, slot_t, gate_t, spread, mod)


def _rope_tables():
    t = np.arange(DEC_SEQ)
    row = (t // GRID_W).astype(np.float32)
    col = (t % GRID_W).astype(np.float32)
    inv = (1.0 / (ROPE_THETA ** (np.arange(0, ROPE_AXIS_DIM, 2, dtype=np.float32) / ROPE_AXIS_DIM))).astype(np.float32)
    d = np.arange(LANES) % HEAD_DIM
    use_col = (d // ROPE_AXIS_DIM) == 1
    second_half = ((d % ROPE_AXIS_DIM) // (ROPE_AXIS_DIM // 2)) == 1
    ang = (np.where(use_col[None, :], col[:, None], row[:, None]) * inv[d % (ROPE_AXIS_DIM // 2)][None, :]).astype(np.float32)
    cos = np.cos(ang).astype(np.float32)
    sin = (np.where(second_half[None, :], 1.0, -1.0) * np.sin(ang)).astype(np.float32)
    return jnp.asarray(cos), jnp.asarray(sin)


def _head_block_ones():
    lane = np.arange(LANES)
    return jnp.asarray((lane[:, None] // HEAD_DIM) == (lane[None, :] // HEAD_DIM), dtype=BF16)


def _spread_tables():
    tiles = N_EXPERTS * SEQ_SLOTS // LANES
    e = np.arange(LANES)[None, :, None]
    lane = np.arange(LANES)[None, None, :]
    tile = np.arange(tiles)[:, None, None]
    return jnp.asarray(e == tile * (LANES // SEQ_SLOTS) + lane // SEQ_SLOTS, dtype=BF16)


def kernel(x_prompt, x_sample, cache_k, cache_v, c, c_ctx, w_mod, b_mod, norm_mix, norm_ffn, attn_w_qkv, attn_q_norm, attn_k_norm, attn_w_o, conv_w_pw1, conv_b_pw1, conv_w_dw, conv_b_dw, conv_ln_g, conv_ln_b, conv_w_pw2, conv_b_pw2, moe_w_router, moe_w_gate, moe_w_up, moe_w_down):
    xs = [x_prompt.reshape(N_PROMPT_TOK, D_MODEL), x_sample.reshape(N_TOK - N_PROMPT_TOK, D_MODEL)]
    cond8 = jnp.concatenate([jnp.broadcast_to(c_ctx[None, :], (N_PROMPT_SUPER, D_MODEL)), c], axis=0)
    mod = _adaln_first(cond8, w_mod, b_mod)

    cos_t, sin_t = _rope_tables()
    bd = _head_block_ones()
    spread = _spread_tables()
    wr_t = jnp.swapaxes(moe_w_router, 1, 2)
    cache_k4 = cache_k.reshape(DEC_BATCH, -1, PAST_LEN, KV_DIM)
    cache_v4 = cache_v.reshape(DEC_BATCH, -1, PAST_LEN, KV_DIM)
    w_o5 = attn_w_o.reshape(-1, N_KV_HEADS, N_GROUPS, HEAD_DIM, D_MODEL)

    new_k, new_v = [], []
    for i in range(DEPTH):
        j = i // N_MIXERS
        g_mix = norm_mix[i][None, :]
        g_ffn = norm_ffn[i][None, :]
        if i % N_MIXERS == 0:
            qg2 = jnp.tile(attn_q_norm[j], LANES // HEAD_DIM)[None, :]
            kg2 = jnp.tile(attn_k_norm[j], LANES // HEAD_DIM)[None, :]
            q, k, v, kc, vc = _qkv(xs, mod, g_mix, attn_w_qkv, j, qg2, kg2, cos_t, sin_t, bd)
            new_k.append(kc[:N_PROMPT_TOK].reshape(BATCH, SEQ, N_KV_HEADS, HEAD_DIM))
            new_v.append(vc[:N_PROMPT_TOK].reshape(BATCH, SEQ, N_KV_HEADS, HEAD_DIM))
            o_p, o_s = _attention(q, k, v, cache_k4, cache_v4, j)
            x = _oproj(xs, o_p, o_s, w_o5, j, mod)
        else:
            x = _conv(xs[0], g_mix, conv_w_pw1, conv_b_pw1[:, None, :], conv_w_dw, conv_b_dw[:, None, :],
                      conv_ln_g[:, None, :], conv_ln_b[:, None, :], conv_w_pw2, conv_b_pw2[:, None, :], mod, j)
        xg, slot_t, gate_t = _moe_select(x, mod, i, g_ffn, wr_t)
        last = i == DEPTH - 1
        ys, next_mod = _moe_ffn(xg, moe_w_gate, moe_w_up, moe_w_down, i,
                                *(() if last else (cond8, w_mod, b_mod)))
        xs = _moe_combine(x, ys, slot_t, gate_t, spread, mod, final=last)
        mod = next_mod

    y_prompt = xs[0].reshape(BATCH, SEQ, D_MODEL)
    y_sample = xs[1].reshape(DEC_BATCH, DEC_SEQ, D_MODEL)
    return (y_prompt, y_sample, jnp.stack(new_k, axis=1), jnp.stack(new_v, axis=1))
```

```python
import functools

import numpy as np

import jax
import jax.numpy as jnp
from jax import lax
from jax.experimental import pallas as pl
from jax.experimental.pallas import tpu as pltpu

F32 = jnp.float32
BF16 = jnp.bfloat16
I32 = jnp.int32

D_MODEL = 1024
BATCH = 16
SEQ = 256
DEPTH = 4
DEC_BATCH = 4
DEC_SEQ = 1024
PAST_LEN = 512
GRID_W = 64
N_MIXERS = 2
HEAD_DIM = 64
N_HEADS = 16
N_KV_HEADS = 4
N_GROUPS = N_HEADS // N_KV_HEADS
KV_DIM = N_KV_HEADS * HEAD_DIM
QKV_DIM = D_MODEL + 2 * KV_DIM
ROPE_AXIS_DIM = HEAD_DIM // 2
ROPE_THETA = 10000.0
CONV_K = 31
N_EXPERTS = 16
EXPERT_FF = 2 * D_MODEL
N_MOD = 6
EPS = 1e-6

N_PROMPT_TOK = BATCH * SEQ
N_TOK = N_PROMPT_TOK + DEC_BATCH * DEC_SEQ
SUPER = 1024
N_SUPER = N_TOK // SUPER
N_PROMPT_SUPER = N_PROMPT_TOK // SUPER
SEQ_PER_SUPER = SUPER // SEQ
SLOTS = SUPER // 8
SEQ_SLOTS = SEQ // 8
LANES = 128
TILE_M = 1024
N_TILES = N_TOK // TILE_M
N_PROMPT_TILES = N_PROMPT_TOK // TILE_M
QKV_SUB = 256
PROJ_SUB = 256
ATTN_TQ = 512
PROMPT_ATTN_SEQS = 8
MOD_COLS = 2 * D_MODEL
MOD_CHUNK = 256
FF_CHUNK = 1024
N_FF_CHUNKS = EXPERT_FF // FF_CHUNK
CONV_ROWS = 64
CONV_HALO = 16
CONV_TAIL_ROWS = 256
GATHER_EXPERTS = 4
SCATTER_EXPERTS = 2
VMEM_LIMIT = 56 * 1024 * 1024


def _dot(a, b):
    return jnp.dot(a, b, preferred_element_type=F32)


def _dot_nt(a, b):
    return lax.dot_general(a, b, (((1,), (1,)), ((), ())), preferred_element_type=F32)


def _sigmoid(x):
    return 1.0 / (1.0 + jnp.exp(-x))


def _norm_mod(x, g, shift, scale):
    y = x * lax.rsqrt(jnp.mean(x * x, axis=-1, keepdims=True) + EPS) * g
    return y * (1.0 + scale) + shift


def _params(semantics):
    return pltpu.CompilerParams(dimension_semantics=semantics, vmem_limit_bytes=VMEM_LIMIT)


def _mod_spec(which, super_of):
    return pl.BlockSpec((1, 3, 1, 1, D_MODEL), lambda *idx: (which, 0, super_of(*idx), 0, 0))


def _tile_super(i):
    return i // (SUPER // TILE_M)


def _x_specs(n_x, rows, n_prompt_blocks):
    if n_x == 1:
        return [pl.BlockSpec((rows, D_MODEL), lambda i: (i, 0))]
    return [pl.BlockSpec((rows, D_MODEL), lambda i: (jnp.minimum(i, n_prompt_blocks - 1), 0)),
            pl.BlockSpec((rows, D_MODEL), lambda i: (jnp.maximum(i - n_prompt_blocks, 0), 0))]


def _adaln_kernel(cond_ref, w_ref, b_ref, o_ref):
    c = cond_ref[...]
    s = (c * _sigmoid(c)).astype(BF16)
    res = _dot(s, w_ref[0].astype(BF16))
    for v in range(MOD_COLS // D_MODEL):
        o_ref[0, v] = res[:, v * D_MODEL:(v + 1) * D_MODEL] + b_ref[0, v]


def _adaln_first(cond8, w_mod, b_mod):
    b4 = b_mod.reshape(DEPTH, N_MOD, 1, D_MODEL)
    per_step = MOD_COLS // D_MODEL
    out = pl.pallas_call(
        _adaln_kernel,
        grid=(N_MOD // per_step,),
        in_specs=[
            pl.BlockSpec((N_SUPER, D_MODEL), lambda n: (0, 0)),
            pl.BlockSpec((1, D_MODEL, MOD_COLS), lambda n: (0, 0, n)),
            pl.BlockSpec((1, per_step, 1, D_MODEL), lambda n: (0, n, 0, 0)),
        ],
        out_specs=pl.BlockSpec((1, per_step, N_SUPER, D_MODEL), lambda n: (0, n, 0, 0)),
        out_shape=jax.ShapeDtypeStruct((1, N_MOD, N_SUPER, D_MODEL), F32),
        compiler_params=_params(("arbitrary",)),
        name="adaln",
    )(cond8, w_mod, b4)
    return out.reshape(2, 3, N_SUPER, 1, D_MODEL)


def _qkv_kernel(*refs, n_x):
    x_refs = refs[:n_x]
    (m_ref, g_ref, w_ref, qg_ref, kg_ref, cos_ref, sin_ref, bd_ref,
     q_ref, k_ref, v_ref, kc_ref, vc_ref, wbf_ref) = refs[n_x:]
    i = pl.program_id(0)

    @pl.when(i == 0)
    def _():
        wbf_ref[...] = w_ref[0].astype(BF16)

    bd = bd_ref[...]
    lane = lax.broadcasted_iota(I32, (QKV_SUB, LANES), 1)
    first_half = (lane & (ROPE_AXIS_DIM // 2)) == 0
    low_head = lane < HEAD_DIM
    latent = i >= N_PROMPT_SUPER
    qg = qg_ref[...] * (HEAD_DIM ** -0.5)
    kg = kg_ref[...]
    g_mix = g_ref[...]
    shift = m_ref[0, 0, 0]
    scale = m_ref[0, 1, 0]

    def head_norm(t, gain):
        sq = t * t
        hi = sq.astype(BF16)
        lo = (sq - hi.astype(F32)).astype(BF16)
        ssum = _dot(hi, bd) + _dot(lo, bd)
        return t * lax.rsqrt(ssum * (1.0 / HEAD_DIM) + EPS) * gain

    for sb in range(SUPER // QKV_SUB):
        rows = pl.ds(sb * QKV_SUB, QKV_SUB)
        x = x_refs[0][rows, :] if n_x == 1 else jnp.where(latent, x_refs[1][rows, :], x_refs[0][rows, :])
        h = _norm_mod(x, g_mix, shift, scale).astype(BF16)
        qkv = _dot(h, wbf_ref[...])
        cos = cos_ref[rows, :]
        sin = sin_ref[rows, :]

        def rope(y):
            partner = jnp.where(first_half, pltpu.roll(y, LANES - ROPE_AXIS_DIM // 2, 1),
                                pltpu.roll(y, ROPE_AXIS_DIM // 2, 1))
            return jnp.where(latent, y * cos + partner * sin, y)

        q_tiles = [rope(head_norm(qkv[:, j * LANES:(j + 1) * LANES], qg)) for j in range(D_MODEL // LANES)]
        for t in range(D_MODEL // LANES):
            grp = t // 2
            src = (4 * (2 * (t % 2)) + grp) // 2
            lo_part = q_tiles[src] if grp % 2 == 0 else pltpu.roll(q_tiles[src], HEAD_DIM, 1)
            hi_part = q_tiles[src + 2] if grp % 2 == 1 else pltpu.roll(q_tiles[src + 2], HEAD_DIM, 1)
            q_ref[rows, t * LANES:(t + 1) * LANES] = jnp.where(low_head, lo_part, hi_part).astype(BF16)
        for j in range(KV_DIM // LANES):
            y = head_norm(qkv[:, D_MODEL + j * LANES:D_MODEL + (j + 1) * LANES], kg)
            k_ref[rows, j * LANES:(j + 1) * LANES] = rope(y).astype(BF16)
            kc_ref[rows, j * LANES:(j + 1) * LANES] = y
        v = qkv[:, D_MODEL + KV_DIM:]
        v_ref[rows, :] = v.astype(BF16)
        vc_ref[rows, :] = v


def _qkv(xs, mod, g, w_qkv, j, qg2, kg2, cos_t, sin_t, bd):
    tile = lambda i: (i, 0)
    const = lambda i: (0, 0)
    return pl.pallas_call(
        functools.partial(_qkv_kernel, n_x=len(xs)),
        grid=(N_SUPER,),
        in_specs=_x_specs(len(xs), SUPER, N_PROMPT_SUPER) + [
            _mod_spec(0, lambda i: i),
            pl.BlockSpec((1, D_MODEL), const),
            pl.BlockSpec((1, D_MODEL, QKV_DIM), lambda i: (j, 0, 0)),
            pl.BlockSpec((1, LANES), const),
            pl.BlockSpec((1, LANES), const),
            pl.BlockSpec((DEC_SEQ, LANES), const),
            pl.BlockSpec((DEC_SEQ, LANES), const),
            pl.BlockSpec((LANES, LANES), const),
        ],
        out_specs=[
            pl.BlockSpec((SUPER, D_MODEL), tile),
            pl.BlockSpec((SUPER, KV_DIM), tile),
            pl.BlockSpec((SUPER, KV_DIM), tile),
            pl.BlockSpec((SUPER, KV_DIM), tile),
            pl.BlockSpec((SUPER, KV_DIM), tile),
        ],
        out_shape=[
            jax.ShapeDtypeStruct((N_TOK, D_MODEL), BF16),
            jax.ShapeDtypeStruct((N_TOK, KV_DIM), BF16),
            jax.ShapeDtypeStruct((N_TOK, KV_DIM), BF16),
            jax.ShapeDtypeStruct((N_TOK, KV_DIM), F32),
            jax.ShapeDtypeStruct((N_TOK, KV_DIM), F32),
        ],
        scratch_shapes=[pltpu.VMEM((D_MODEL, QKV_DIM), BF16)],
        compiler_params=_params(("arbitrary",)),
        name="qkv_proj",
    )(*xs, mod, g, w_qkv, qg2, kg2, cos_t, sin_t, bd)


def _attn_heads(q_ref, rows, n_rows, k_all, v_all, o_ref):
    lane = lax.broadcasted_iota(I32, (n_rows, KV_DIM), 1)
    for g in range(N_GROUPS):
        qs = q_ref[rows, g * KV_DIM:(g + 1) * KV_DIM]
        acc = jnp.zeros((n_rows, KV_DIM), F32)
        for kv in range(N_KV_HEADS):
            mask = (lane >= kv * HEAD_DIM) & (lane < (kv + 1) * HEAD_DIM)
            s = _dot_nt(jnp.where(mask, qs, jnp.zeros_like(qs)), k_all)
            p = jnp.exp(s - jnp.max(s, axis=-1, keepdims=True))
            inv = 1.0 / jnp.sum(p, axis=-1, keepdims=True)
            o = _dot(p.astype(BF16), v_all)
            acc = acc + jnp.where(mask, o * inv, 0.0)
        o_ref[rows, g * KV_DIM:(g + 1) * KV_DIM] = acc.astype(BF16)


def _attn_prompt_kernel(q_ref, k_ref, v_ref, o_ref):
    for r in range(PROMPT_ATTN_SEQS):
        rows = pl.ds(r * SEQ, SEQ)
        _attn_heads(q_ref, rows, SEQ, k_ref[rows, :], v_ref[rows, :], o_ref)


def _attn_sample_kernel(q_ref, k_ref, v_ref, ck_ref, cv_ref, o_ref, kall_ref, vall_ref):
    @pl.when(pl.program_id(1) == 0)
    def _():
        kall_ref[:PAST_LEN] = ck_ref[0, 0].astype(BF16)
        kall_ref[PAST_LEN:] = k_ref[...]
        vall_ref[:PAST_LEN] = cv_ref[0, 0].astype(BF16)
        vall_ref[PAST_LEN:] = v_ref[...]

    _attn_heads(q_ref, pl.ds(0, ATTN_TQ), ATTN_TQ, kall_ref[...], vall_ref[...], o_ref)


def _attention(q, k, v, cache_k4, cache_v4, j):
    p_rows = PROMPT_ATTN_SEQS * SEQ
    o_p = pl.pallas_call(
        _attn_prompt_kernel,
        grid=(BATCH // PROMPT_ATTN_SEQS,),
        in_specs=[
            pl.BlockSpec((p_rows, D_MODEL), lambda b: (b, 0)),
            pl.BlockSpec((p_rows, KV_DIM), lambda b: (b, 0)),
            pl.BlockSpec((p_rows, KV_DIM), lambda b: (b, 0)),
        ],
        out_specs=pl.BlockSpec((p_rows, D_MODEL), lambda b: (b, 0)),
        out_shape=jax.ShapeDtypeStruct((N_PROMPT_TOK, D_MODEL), BF16),
        compiler_params=_params(("arbitrary",)),
        name="attn_prompt",
    )(q, k, v)
    q_blocks = DEC_SEQ // ATTN_TQ
    first_q = N_PROMPT_TOK // ATTN_TQ
    first_k = N_PROMPT_TOK // DEC_SEQ
    o_s = pl.pallas_call(
        _attn_sample_kernel,
        grid=(DEC_BATCH, q_blocks),
        in_specs=[
            pl.BlockSpec((ATTN_TQ, D_MODEL), lambda b, t: (first_q + b * q_blocks + t, 0)),
            pl.BlockSpec((DEC_SEQ, KV_DIM), lambda b, t: (first_k + b, 0)),
            pl.BlockSpec((DEC_SEQ, KV_DIM), lambda b, t: (first_k + b, 0)),
            pl.BlockSpec((1, 1, PAST_LEN, KV_DIM), lambda b, t: (b, j, 0, 0)),
            pl.BlockSpec((1, 1, PAST_LEN, KV_DIM), lambda b, t: (b, j, 0, 0)),
        ],
        out_specs=pl.BlockSpec((ATTN_TQ, D_MODEL), lambda b, t: (b * q_blocks + t, 0)),
        out_shape=jax.ShapeDtypeStruct((DEC_BATCH * DEC_SEQ, D_MODEL), BF16),
        scratch_shapes=[pltpu.VMEM((PAST_LEN + DEC_SEQ, KV_DIM), BF16),
                        pltpu.VMEM((PAST_LEN + DEC_SEQ, KV_DIM), BF16)],
        compiler_params=_params(("arbitrary", "arbitrary")),
        name="attn_sample",
    )(q, k, v, cache_k4, cache_v4)
    return o_p, o_s


def _oproj_kernel(*refs, n_x):
    x_refs = refs[:n_x]
    op_ref, os_ref = refs[n_x:n_x + 2]
    w_refs = refs[n_x + 2:n_x + 2 + N_GROUPS]
    m_ref, out_ref, wbf_ref = refs[n_x + 2 + N_GROUPS:]
    i = pl.program_id(0)

    @pl.when(i == 0)
    def _():
        for g in range(N_GROUPS):
            wbf_ref[g * KV_DIM:(g + 1) * KV_DIM, :] = w_refs[g][0, :, 0].reshape(KV_DIM, D_MODEL).astype(BF16)

    prompt = i < N_PROMPT_TILES
    gate = m_ref[0, 2, 0]
    for sb in range(TILE_M // PROJ_SUB):
        rows = pl.ds(sb * PROJ_SUB, PROJ_SUB)
        o = jnp.where(prompt, op_ref[rows, :], os_ref[rows, :])
        x = x_refs[0][rows, :] if n_x == 1 else jnp.where(prompt, x_refs[0][rows, :], x_refs[1][rows, :])
        out_ref[rows, :] = x + gate * _dot(o, wbf_ref[...])


def _oproj(xs, o_p, o_s, w_o5, j, mod):
    w_specs = [pl.BlockSpec((1, N_KV_HEADS, 1, HEAD_DIM, D_MODEL), lambda i, g=g: (j, 0, g, 0, 0))
               for g in range(N_GROUPS)]
    return pl.pallas_call(
        functools.partial(_oproj_kernel, n_x=len(xs)),
        grid=(N_TILES,),
        in_specs=_x_specs(len(xs), TILE_M, N_PROMPT_TILES) + [
            pl.BlockSpec((TILE_M, D_MODEL), lambda i: (jnp.minimum(i, N_PROMPT_TILES - 1), 0)),
            pl.BlockSpec((TILE_M, D_MODEL), lambda i: (jnp.maximum(i - N_PROMPT_TILES, 0), 0)),
        ] + w_specs + [_mod_spec(0, _tile_super)],
        out_specs=pl.BlockSpec((TILE_M, D_MODEL), lambda i: (i, 0)),
        out_shape=jax.ShapeDtypeStruct((N_TOK, D_MODEL), F32),
        scratch_shapes=[pltpu.VMEM((D_MODEL, D_MODEL), BF16)],
        compiler_params=_params(("arbitrary",)),
        name="attn_out_proj",
    )(*xs, o_p, o_s, *([w_o5] * N_GROUPS), mod)


PAD_ROWS = CONV_HALO + SEQ_PER_SUPER * (SEQ + CONV_HALO)


def _conv_kernel(x_ref, g_ref, w1_ref, b1_ref, wdw_ref, bdw_ref, lng_ref, lnb_ref, w2_ref, b2_ref, m_ref,
                 out_ref, pad_ref, cv_ref, w1bf_ref, wbf_ref):
    ss = pl.program_id(0)

    @pl.when(ss == 0)
    def _():
        w1bf_ref[...] = w1_ref[0].astype(BF16)
        wbf_ref[...] = w2_ref[0].astype(BF16)

    seq_gap = jnp.where(ss < N_PROMPT_SUPER, CONV_HALO, 0)
    halo = jnp.zeros((CONV_HALO, LANES), F32)
    for j in range(D_MODEL // LANES):
        pad_ref[j, pl.ds(0, CONV_HALO), :] = halo
        for s in range(SEQ_PER_SUPER):
            pad_ref[j, pl.ds(CONV_HALO + (s + 1) * SEQ + s * seq_gap, CONV_HALO), :] = halo
    g_mix = g_ref[...]
    shift = m_ref[0, 0, 0]
    scale = m_ref[0, 1, 0]
    for s in range(SEQ_PER_SUPER):
        h = _norm_mod(x_ref[pl.ds(s * SEQ, SEQ), :], g_mix, shift, scale).astype(BF16)
        z = _dot(h, w1bf_ref[...]) + b1_ref[0]
        u = z[:, :D_MODEL] * _sigmoid(z[:, D_MODEL:])
        for j in range(D_MODEL // LANES):
            pad_ref[j, pl.ds(CONV_HALO + s * SEQ + s * seq_gap, SEQ), :] = u[:, j * LANES:(j + 1) * LANES]

    def block(c, carry):
        r0 = pl.multiple_of(c * CONV_ROWS, CONV_ROWS)
        base = r0 + (c // (SEQ // CONV_ROWS)) * seq_gap + (CONV_HALO - CONV_K // 2)
        for j in range(D_MODEL // LANES):
            cols = slice(j * LANES, (j + 1) * LANES)
            acc = jnp.zeros((CONV_ROWS, LANES), F32)
            for kk in range(CONV_K):
                acc = acc + wdw_ref[0, pl.ds(kk, 1), cols] * pad_ref[j, pl.ds(base + kk, CONV_ROWS), :]
            cv_ref[pl.ds(r0, CONV_ROWS), cols] = acc + bdw_ref[0, :, cols]
        return carry

    lax.fori_loop(0, SUPER // CONV_ROWS, block, 0)

    gate = m_ref[0, 2, 0]
    for sb in range(SUPER // CONV_TAIL_ROWS):
        rows = pl.ds(sb * CONV_TAIL_ROWS, CONV_TAIL_ROWS)
        cv = cv_ref[rows, :]
        mu = jnp.mean(cv, axis=-1, keepdims=True)
        cen = cv - mu
        var = jnp.mean(cen * cen, axis=-1, keepdims=True)
        y = cen * lax.rsqrt(var + EPS) * lng_ref[0] + lnb_ref[0]
        act = (y * _sigmoid(y)).astype(BF16)
        out_ref[rows, :] = x_ref[rows, :] + gate * (_dot(act, wbf_ref[...]) + b2_ref[0])


def _conv(x, g, w1, b1, wdw, bdw, lng, lnb, w2, b2, mod, j):
    vec = pl.BlockSpec((1, 1, D_MODEL), lambda s: (j, 0, 0))
    return pl.pallas_call(
        _conv_kernel,
        grid=(N_SUPER,),
        in_specs=[
            pl.BlockSpec((SUPER, D_MODEL), lambda s: (s, 0)),
            pl.BlockSpec((1, D_MODEL), lambda s: (0, 0)),
            pl.BlockSpec((1, D_MODEL, 2 * D_MODEL), lambda s: (j, 0, 0)),
            pl.BlockSpec((1, 1, 2 * D_MODEL), lambda s: (j, 0, 0)),
            pl.BlockSpec((1, CONV_K, D_MODEL), lambda s: (j, 0, 0)),
            vec, vec, vec,
            pl.BlockSpec((1, D_MODEL, D_MODEL), lambda s: (j, 0, 0)),
            vec,
            _mod_spec(0, lambda s: s),
        ],
        out_specs=pl.BlockSpec((SUPER, D_MODEL), lambda s: (s, 0)),
        out_shape=jax.ShapeDtypeStruct((N_TOK, D_MODEL), F32),
        scratch_shapes=[pltpu.VMEM((D_MODEL // LANES, PAD_ROWS, LANES), F32),
                        pltpu.VMEM((SUPER, D_MODEL), F32),
                        pltpu.VMEM((D_MODEL, 2 * D_MODEL), BF16),
                        pltpu.VMEM((D_MODEL, D_MODEL), BF16)],
        compiler_params=_params(("arbitrary",)),
        name="conv_module",
    )(x, g, w1, b1, wdw, bdw, lng, lnb, w2, b2, mod)


def _route(aff, cap, tri):
    rows = aff.shape[0]
    capf = float(cap)

    def count_ge(bits):
        return jnp.sum((aff >= lax.bitcast_convert_type(bits, F32)).astype(F32), axis=1, keepdims=True)

    thr = jnp.zeros((rows, 1), I32)
    top = thr | (1 << 30)
    thr = jnp.where(count_ge(top) >= capf, top, thr)
    for shift in range(27, -1, -3):
        digit = jnp.zeros((rows, 1), I32)
        for d in range(1, 8):
            digit = digit + (count_ge(thr | (d << shift)) >= capf).astype(I32)
        thr = thr | (digit << shift)
    thr_f = lax.bitcast_convert_type(thr, F32)
    above = aff > thr_f
    tied = aff == thr_f
    need = capf - jnp.sum(above.astype(F32), axis=1, keepdims=True)
    tied_rank = _dot(tied.astype(BF16), tri)
    sel = above | (tied & (tied_rank <= need))
    return jnp.where(sel, _dot(sel.astype(BF16), tri) - 1.0, -1.0)


def _pad_rows_t(a, fill):
    pad = jnp.full((LANES - N_EXPERTS, a.shape[1]), fill, F32)
    return jnp.concatenate([a, pad], axis=0).T


def _moe_select_kernel(x_ref, m_ref, g_ref, wrt_ref, xs_ref, slot_ref, gate_ref, h_ref, tri_ref, p_ref):
    ss = pl.program_id(0)

    @pl.when(ss == 0)
    def _():
        r = lax.broadcasted_iota(I32, (SUPER, SUPER), 0)
        c = lax.broadcasted_iota(I32, (SUPER, SUPER), 1)
        tri_ref[...] = (r <= c).astype(BF16)

    h_ref[...] = _norm_mod(x_ref[...], g_ref[...], m_ref[0, 0, 0], m_ref[0, 1, 0]).astype(BF16)

    logit = _dot_nt(wrt_ref[0].astype(BF16), h_ref[...])
    ex = jnp.exp(logit - jnp.max(logit, axis=0, keepdims=True))
    aff = ex / jnp.sum(ex, axis=0, keepdims=True)

    @pl.when(ss < N_PROMPT_SUPER)
    def _():
        aff_r = jnp.concatenate([aff[:, b * SEQ:(b + 1) * SEQ] for b in range(SEQ_PER_SUPER)], axis=0)
        pos = _route(aff_r, SEQ_SLOTS, tri_ref[:SEQ, :SEQ])
        sub = lax.broadcasted_iota(I32, (SEQ_SLOTS, SEQ), 0).astype(F32)
        for b in range(SEQ_PER_SUPER):
            for e in range(N_EXPERTS):
                r = b * N_EXPERTS + e
                p_ref[e * SEQ_SLOTS:(e + 1) * SEQ_SLOTS, :] = (sub == pos[r:r + 1, :]).astype(BF16)
            rows = _dot(p_ref[...], h_ref[b * SEQ:(b + 1) * SEQ, :])
            xs_ref[:, b * SEQ_SLOTS:(b + 1) * SEQ_SLOTS, :] = (
                rows.astype(BF16).reshape(N_EXPERTS, SEQ_SLOTS, D_MODEL))
            slot_ref[b * SEQ:(b + 1) * SEQ, :] = _pad_rows_t(pos[b * N_EXPERTS:(b + 1) * N_EXPERTS], -1.0)
            gate_ref[b * SEQ:(b + 1) * SEQ, :] = _pad_rows_t(aff[:, b * SEQ:(b + 1) * SEQ], 0.0)

    @pl.when(ss >= N_PROMPT_SUPER)
    def _():
        pos = _route(aff, SLOTS, tri_ref[...])
        sub = lax.broadcasted_iota(I32, (SLOTS, SUPER), 0).astype(F32)
        for e0 in range(0, N_EXPERTS, GATHER_EXPERTS):
            onehot = jnp.concatenate([(sub == pos[e:e + 1, :]).astype(BF16)
                                      for e in range(e0, e0 + GATHER_EXPERTS)], axis=0)
            rows = _dot(onehot, h_ref[...])
            xs_ref[e0:e0 + GATHER_EXPERTS] = rows.astype(BF16).reshape(GATHER_EXPERTS, SLOTS, D_MODEL)
        slot_ref[...] = _pad_rows_t(pos, -1.0)
        gate_ref[...] = _pad_rows_t(aff, 0.0)


def _moe_select(x, mod, layer, g, wr_t):
    return pl.pallas_call(
        _moe_select_kernel,
        grid=(N_SUPER,),
        in_specs=[
            pl.BlockSpec((SUPER, D_MODEL), lambda s: (s, 0)),
            _mod_spec(1, lambda s: s),
            pl.BlockSpec((1, D_MODEL), lambda s: (0, 0)),
            pl.BlockSpec((1, N_EXPERTS, D_MODEL), lambda s: (layer, 0, 0)),
        ],
        out_specs=[
            pl.BlockSpec((N_EXPERTS, SLOTS, D_MODEL), lambda s: (0, s, 0)),
            pl.BlockSpec((SUPER, LANES), lambda s: (s, 0)),
            pl.BlockSpec((SUPER, LANES), lambda s: (s, 0)),
        ],
        out_shape=[
            jax.ShapeDtypeStruct((N_EXPERTS, N_SUPER * SLOTS, D_MODEL), BF16),
            jax.ShapeDtypeStruct((N_TOK, LANES), F32),
            jax.ShapeDtypeStruct((N_TOK, LANES), F32),
        ],
        scratch_shapes=[pltpu.VMEM((SUPER, D_MODEL), BF16),
                        pltpu.VMEM((SUPER, SUPER), BF16),
                        pltpu.VMEM((N_EXPERTS * SEQ_SLOTS, SEQ), BF16)],
        compiler_params=_params(("arbitrary",)),
        name="moe_select",
    )(x, mod, g, wr_t)


def _moe_ffn_kernel(x_ref, wg_ref, wu_ref, wd_ref, *refs, with_mod):
    if with_mod:
        cond_ref, wm_ref, bm_ref, y_ref, mod_ref, acc_ref = refs
        c = cond_ref[...]
        s = (c * _sigmoid(c)).astype(BF16)
        mod_ref[0] = _dot(s, wm_ref[0].astype(BF16)) + bm_ref[0, 0]
    else:
        y_ref, acc_ref = refs
    f = pl.program_id(1)
    x = x_ref[0]
    a = _dot(x, wg_ref[0, 0].astype(BF16))
    u = _dot(x, wu_ref[0, 0].astype(BF16))
    hid = (a * _sigmoid(a) * u).astype(BF16)
    acc_ref[f] = _dot(hid, wd_ref[0, 0].astype(BF16))

    @pl.when(f == N_FF_CHUNKS - 1)
    def _():
        y_ref[0] = (acc_ref[0] + acc_ref[1]).astype(BF16)


def _moe_ffn(xs, w_gate, w_up, w_down, layer, cond8=None, w_mod=None, b_mod=None):
    assert N_FF_CHUNKS == 2
    rows = N_SUPER * SLOTS
    with_mod = cond8 is not None
    in_specs = [
        pl.BlockSpec((1, rows, D_MODEL), lambda e, f: (e, 0, 0)),
        pl.BlockSpec((1, 1, D_MODEL, FF_CHUNK), lambda e, f: (layer, e, 0, f)),
        pl.BlockSpec((1, 1, D_MODEL, FF_CHUNK), lambda e, f: (layer, e, 0, f)),
        pl.BlockSpec((1, 1, FF_CHUNK, D_MODEL), lambda e, f: (layer, e, f, 0)),
    ]
    out_specs = [pl.BlockSpec((1, rows, D_MODEL), lambda e, f: (e, 0, 0))]
    out_shape = [jax.ShapeDtypeStruct((N_EXPERTS, rows, D_MODEL), BF16)]
    args = [xs, w_gate, w_up, w_down]
    if with_mod:
        n_chunks = N_MOD * D_MODEL // MOD_CHUNK
        assert n_chunks <= N_EXPERTS * N_FF_CHUNKS
        chunk = lambda e, f: jnp.minimum(e * N_FF_CHUNKS + f, n_chunks - 1)
        in_specs += [
            pl.BlockSpec((N_SUPER, D_MODEL), lambda e, f: (0, 0)),
            pl.BlockSpec((1, D_MODEL, MOD_CHUNK), lambda e, f: (layer + 1, 0, chunk(e, f))),
            pl.BlockSpec((1, 1, 1, MOD_CHUNK), lambda e, f: (layer + 1, chunk(e, f), 0, 0)),
        ]
        out_specs.append(pl.BlockSpec((1, N_SUPER, MOD_CHUNK), lambda e, f: (chunk(e, f), 0, 0)))
        out_shape.append(jax.ShapeDtypeStruct((n_chunks, N_SUPER, MOD_CHUNK), F32))
        args += [cond8, w_mod, b_mod.reshape(DEPTH, n_chunks, 1, MOD_CHUNK)]
    outs = pl.pallas_call(
        functools.partial(_moe_ffn_kernel, with_mod=with_mod),
        grid=(N_EXPERTS, N_FF_CHUNKS),
        in_specs=in_specs,
        out_specs=out_specs,
        out_shape=out_shape,
        scratch_shapes=[pltpu.VMEM((N_FF_CHUNKS, rows, D_MODEL), F32)],
        compiler_params=_params(("arbitrary", "arbitrary")),
        name="moe_ffn",
    )(*args)
    if not with_mod:
        return outs[0], None
    mod = outs[1].transpose(1, 0, 2).reshape(N_SUPER, N_MOD, D_MODEL).transpose(1, 0, 2)
    return outs[0], mod.reshape(2, 3, N_SUPER, 1, D_MODEL)


def _moe_combine_kernel(x_ref, y_ref, slot_ref, gate_ref, spread_ref, m_ref, *refs, final):
    out_refs, s_ref = refs[:-1], refs[-1]
    ss = pl.program_id(0)
    gate_f = m_ref[0, 2, 0]

    @pl.when(ss < N_PROMPT_SUPER)
    def _():
        lane_slot = (lax.broadcasted_iota(I32, (SEQ, LANES), 1) % SEQ_SLOTS).astype(F32)
        for b in range(SEQ_PER_SUPER):
            rows = pl.ds(b * SEQ, SEQ)
            slot_b = slot_ref[rows, :].astype(BF16)
            gate_b = gate_ref[rows, :].astype(BF16)
            for t in range(N_EXPERTS * SEQ_SLOTS // LANES):
                hit = _dot(slot_b, spread_ref[t]) == lane_slot
                s_ref[rows, t * LANES:(t + 1) * LANES] = jnp.where(hit, _dot(gate_b, spread_ref[t]), 0.0).astype(BF16)
            y_b = y_ref[:, b * SEQ_SLOTS:(b + 1) * SEQ_SLOTS, :].reshape(N_EXPERTS * SEQ_SLOTS, D_MODEL)
            res = x_ref[rows, :] + gate_f * _dot(s_ref[rows, :N_EXPERTS * SEQ_SLOTS], y_b)
            out_refs[0][rows, :] = res

    @pl.when(ss >= N_PROMPT_SUPER)
    def _():
        lane = lax.broadcasted_iota(I32, (SUPER, SLOTS), 1).astype(F32)
        moe = None
        for e0 in range(0, N_EXPERTS, SCATTER_EXPERTS):
            s_cols = jnp.concatenate(
                [jnp.where(slot_ref[:, e:e + 1] == lane, gate_ref[:, e:e + 1], 0.0).astype(BF16)
                 for e in range(e0, e0 + SCATTER_EXPERTS)], axis=1)
            part = _dot(s_cols, y_ref[e0:e0 + SCATTER_EXPERTS].reshape(SCATTER_EXPERTS * SLOTS, D_MODEL))
            moe = part if moe is None else moe + part
        out_refs[-1][...] = x_ref[...] + gate_f * moe


def _moe_combine(x, ys, slot_t, gate_t, spread, mod, final):
    if final:
        out_specs = [pl.BlockSpec((SUPER, D_MODEL), lambda s: (jnp.minimum(s, N_PROMPT_SUPER - 1), 0)),
                     pl.BlockSpec((SUPER, D_MODEL), lambda s: (jnp.maximum(s - N_PROMPT_SUPER, 0), 0))]
        out_shape = [jax.ShapeDtypeStruct((N_PROMPT_TOK, D_MODEL), F32),
                     jax.ShapeDtypeStruct((N_TOK - N_PROMPT_TOK, D_MODEL), F32)]
    else:
        out_specs = [pl.BlockSpec((SUPER, D_MODEL), lambda s: (s, 0))]
        out_shape = [jax.ShapeDtypeStruct((N_TOK, D_MODEL), F32)]
    return pl.pallas_call(
        functools.partial(_moe_combine_kernel, final=final),
        grid=(N_SUPER,),
        in_specs=[
            pl.BlockSpec((SUPER, D_MODEL), lambda s: (s, 0)),
            pl.BlockSpec((N_EXPERTS, SLOTS, D_MODEL), lambda s: (0, s, 0)),
            pl.BlockSpec((SUPER, LANES), lambda s: (s, 0)),
            pl.BlockSpec((SUPER, LANES), lambda s: (s, 0)),
            pl.BlockSpec((N_EXPERTS * SEQ_SLOTS // LANES, LANES, LANES), lambda s: (0, 0, 0)),
            _mod_spec(1, lambda s: s),
        ],
        out_specs=out_specs,
        out_shape=out_shape,
        scratch_shapes=[pltpu.VMEM((SUPER, N_EXPERTS * SEQ_SLOTS), BF16)],
        compiler_params=_params(("arbitrary",)),
        name="moe_combine",
    )(x, ys, slot_t, gate_t, spread, mod)


def _rope_tables():
    t = np.arange(DEC_SEQ)
    row = (t // GRID_W).astype(np.float32)
    col = (t % GRID_W).astype(np.float32)
    inv = (1.0 / (ROPE_THETA ** (np.arange(0, ROPE_AXIS_DIM, 2, dtype=np.float32) / ROPE_AXIS_DIM))).astype(np.float32)
    d = np.arange(LANES) % HEAD_DIM
    use_col = (d // ROPE_AXIS_DIM) == 1
    second_half = ((d % ROPE_AXIS_DIM) // (ROPE_AXIS_DIM // 2)) == 1
    ang = (np.where(use_col[None, :], col[:, None], row[:, None]) * inv[d % (ROPE_AXIS_DIM // 2)][None, :]).astype(np.float32)
    cos = np.cos(ang).astype(np.float32)
    sin = (np.where(second_half[None, :], 1.0, -1.0) * np.sin(ang)).astype(np.float32)
    return jnp.asarray(cos), jnp.asarray(sin)


def _head_block_ones():
    lane = np.arange(LANES)
    return jnp.asarray((lane[:, None] // HEAD_DIM) == (lane[None, :] // HEAD_DIM), dtype=BF16)


def _spread_tables():
    tiles = N_EXPERTS * SEQ_SLOTS // LANES
    e = np.arange(LANES)[None, :, None]
    lane = np.arange(LANES)[None, None, :]
    tile = np.arange(tiles)[:, None, None]
    return jnp.asarray(e == tile * (LANES // SEQ_SLOTS) + lane // SEQ_SLOTS, dtype=BF16)


def kernel(x_prompt, x_sample, cache_k, cache_v, c, c_ctx, w_mod, b_mod, norm_mix, norm_ffn, attn_w_qkv, attn_q_norm, attn_k_norm, attn_w_o, conv_w_pw1, conv_b_pw1, conv_w_dw, conv_b_dw, conv_ln_g, conv_ln_b, conv_w_pw2, conv_b_pw2, moe_w_router, moe_w_gate, moe_w_up, moe_w_down):
    xs = [x_prompt.reshape(N_PROMPT_TOK, D_MODEL), x_sample.reshape(N_TOK - N_PROMPT_TOK, D_MODEL)]
    cond8 = jnp.concatenate([jnp.broadcast_to(c_ctx[None, :], (N_PROMPT_SUPER, D_MODEL)), c], axis=0)
    mod = _adaln_first(cond8, w_mod, b_mod)

    cos_t, sin_t = _rope_tables()
    bd = _head_block_ones()
    spread = _spread_tables()
    wr_t = jnp.swapaxes(moe_w_router, 1, 2)
    cache_k4 = cache_k.reshape(DEC_BATCH, -1, PAST_LEN, KV_DIM)
    cache_v4 = cache_v.reshape(DEC_BATCH, -1, PAST_LEN, KV_DIM)
    w_o5 = attn_w_o.reshape(-1, N_KV_HEADS, N_GROUPS, HEAD_DIM, D_MODEL)

    new_k, new_v = [], []
    for i in range(DEPTH):
        j = i // N_MIXERS
        g_mix = norm_mix[i][None, :]
        g_ffn = norm_ffn[i][None, :]
        if i % N_MIXERS == 0:
            qg2 = jnp.tile(attn_q_norm[j], LANES // HEAD_DIM)[None, :]
            kg2 = jnp.tile(attn_k_norm[j], LANES // HEAD_DIM)[None, :]
            q, k, v, kc, vc = _qkv(xs, mod, g_mix, attn_w_qkv, j, qg2, kg2, cos_t, sin_t, bd)
            new_k.append(kc[:N_PROMPT_TOK].reshape(BATCH, SEQ, N_KV_HEADS, HEAD_DIM))
            new_v.append(vc[:N_PROMPT_TOK].reshape(BATCH, SEQ, N_KV_HEADS, HEAD_DIM))
            o_p, o_s = _attention(q, k, v, cache_k4, cache_v4, j)
            x = _oproj(xs, o_p, o_s, w_o5, j, mod)
        else:
            x = _conv(xs[0], g_mix, conv_w_pw1, conv_b_pw1[:, None, :], conv_w_dw, conv_b_dw[:, None, :],
                      conv_ln_g[:, None, :], conv_ln_b[:, None, :], conv_w_pw2, conv_b_pw2[:, None, :], mod, j)
        xg, slot_t, gate_t = _moe_select(x, mod, i, g_ffn, wr_t)
        last = i == DEPTH - 1
        ys, next_mod = _moe_ffn(xg, moe_w_gate, moe_w_up, moe_w_down, i,
                                *(() if last else (cond8, w_mod, b_mod)))
        xs = _moe_combine(x, ys, slot_t, gate_t, spread, mod, final=last)
        mod = next_mod

    y_prompt = xs[0].reshape(BATCH, SEQ, D_MODEL)
    y_sample = xs[1].reshape(DEC_BATCH, DEC_SEQ, D_MODEL)
    return (y_prompt, y_sample, jnp.stack(new_k, axis=1), jnp.stack(new_v, axis=1))
```

```python
import functools

import numpy as np

import jax
import jax.numpy as jnp
from jax import lax
from jax.experimental import pallas as pl
from jax.experimental.pallas import tpu as pltpu

F32 = jnp.float32
BF16 = jnp.bfloat16
I32 = jnp.int32

D_MODEL = 1024
BATCH = 16
SEQ = 256
DEPTH = 4
DEC_BATCH = 4
DEC_SEQ = 1024
PAST_LEN = 512
GRID_W = 64
N_MIXERS = 2
HEAD_DIM = 64
N_HEADS = 16
N_KV_HEADS = 4
N_GROUPS = N_HEADS // N_KV_HEADS
KV_DIM = N_KV_HEADS * HEAD_DIM
QKV_DIM = D_MODEL + 2 * KV_DIM
ROPE_AXIS_DIM = HEAD_DIM // 2
ROPE_THETA = 10000.0
CONV_K = 31
N_EXPERTS = 16
EXPERT_FF = 2 * D_MODEL
N_MOD = 6
EPS = 1e-6

N_PROMPT_TOK = BATCH * SEQ
N_TOK = N_PROMPT_TOK + DEC_BATCH * DEC_SEQ
SUPER = 1024
N_SUPER = N_TOK // SUPER
N_PROMPT_SUPER = N_PROMPT_TOK // SUPER
SEQ_PER_SUPER = SUPER // SEQ
SLOTS = SUPER // 8
SEQ_SLOTS = SEQ // 8
LANES = 128
TILE_M = 1024
N_TILES = N_TOK // TILE_M
N_PROMPT_TILES = N_PROMPT_TOK // TILE_M
QKV_SUB = 256
PROJ_SUB = 256
ATTN_TQ = 512
PROMPT_ATTN_SEQS = 4
MOD_COLS = 2 * D_MODEL
MOD_CHUNK = 256
FF_CHUNK = 1024
N_FF_CHUNKS = EXPERT_FF // FF_CHUNK
CONV_ROWS = 64
CONV_HALO = 16
CONV_TAIL_ROWS = 256
GATHER_EXPERTS = 4
SCATTER_EXPERTS = 2
VMEM_LIMIT = 56 * 1024 * 1024


def _dot(a, b):
    return jnp.dot(a, b, preferred_element_type=F32)


def _dot_nt(a, b):
    return lax.dot_general(a, b, (((1,), (1,)), ((), ())), preferred_element_type=F32)


def _sigmoid(x):
    return 1.0 / (1.0 + jnp.exp(-x))


def _norm_mod(x, g, shift, scale):
    y = x * lax.rsqrt(jnp.mean(x * x, axis=-1, keepdims=True) + EPS) * g
    return y * (1.0 + scale) + shift


def _params(semantics):
    return pltpu.CompilerParams(dimension_semantics=semantics, vmem_limit_bytes=VMEM_LIMIT)


def _mod_spec(which, super_of):
    return pl.BlockSpec((1, 3, 1, 1, D_MODEL), lambda *idx: (which, 0, super_of(*idx), 0, 0))


def _tile_super(i):
    return i // (SUPER // TILE_M)


def _x_specs(n_x, rows, n_prompt_blocks):
    if n_x == 1:
        return [pl.BlockSpec((rows, D_MODEL), lambda i: (i, 0))]
    return [pl.BlockSpec((rows, D_MODEL), lambda i: (jnp.minimum(i, n_prompt_blocks - 1), 0)),
            pl.BlockSpec((rows, D_MODEL), lambda i: (jnp.maximum(i - n_prompt_blocks, 0), 0))]


def _adaln_kernel(cond_ref, w_ref, b_ref, o_ref):
    c = cond_ref[...]
    s = (c * _sigmoid(c)).astype(BF16)
    res = _dot(s, w_ref[0].astype(BF16))
    for v in range(MOD_COLS // D_MODEL):
        o_ref[0, v] = res[:, v * D_MODEL:(v + 1) * D_MODEL] + b_ref[0, v]


def _adaln_first(cond8, w_mod, b_mod):
    b4 = b_mod.reshape(DEPTH, N_MOD, 1, D_MODEL)
    per_step = MOD_COLS // D_MODEL
    out = pl.pallas_call(
        _adaln_kernel,
        grid=(N_MOD // per_step,),
        in_specs=[
            pl.BlockSpec((N_SUPER, D_MODEL), lambda n: (0, 0)),
            pl.BlockSpec((1, D_MODEL, MOD_COLS), lambda n: (0, 0, n)),
            pl.BlockSpec((1, per_step, 1, D_MODEL), lambda n: (0, n, 0, 0)),
        ],
        out_specs=pl.BlockSpec((1, per_step, N_SUPER, D_MODEL), lambda n: (0, n, 0, 0)),
        out_shape=jax.ShapeDtypeStruct((1, N_MOD, N_SUPER, D_MODEL), F32),
        compiler_params=_params(("arbitrary",)),
        name="adaln",
    )(cond8, w_mod, b4)
    return out.reshape(2, 3, N_SUPER, 1, D_MODEL)


def _qkv_kernel(*refs, n_x):
    x_refs = refs[:n_x]
    (m_ref, g_ref, w_ref, qg_ref, kg_ref, cos_ref, sin_ref, bd_ref,
     q_ref, k_ref, v_ref, kc_ref, vc_ref, wbf_ref) = refs[n_x:]
    i = pl.program_id(0)

    @pl.when(i == 0)
    def _():
        wbf_ref[...] = w_ref[0].astype(BF16)

    bd = bd_ref[...]
    lane = lax.broadcasted_iota(I32, (QKV_SUB, LANES), 1)
    first_half = (lane & (ROPE_AXIS_DIM // 2)) == 0
    low_head = lane < HEAD_DIM
    latent = i >= N_PROMPT_SUPER
    qg = qg_ref[...] * (HEAD_DIM ** -0.5)
    kg = kg_ref[...]
    g_mix = g_ref[...]
    shift = m_ref[0, 0, 0]
    scale = m_ref[0, 1, 0]

    def head_norm(t, gain):
        sq = t * t
        hi = sq.astype(BF16)
        lo = (sq - hi.astype(F32)).astype(BF16)
        ssum = _dot(hi, bd) + _dot(lo, bd)
        return t * lax.rsqrt(ssum * (1.0 / HEAD_DIM) + EPS) * gain

    for sb in range(SUPER // QKV_SUB):
        rows = pl.ds(sb * QKV_SUB, QKV_SUB)
        x = x_refs[0][rows, :] if n_x == 1 else jnp.where(latent, x_refs[1][rows, :], x_refs[0][rows, :])
        h = _norm_mod(x, g_mix, shift, scale).astype(BF16)
        qkv = _dot(h, wbf_ref[...])
        cos = cos_ref[rows, :]
        sin = sin_ref[rows, :]

        def rope(y):
            partner = jnp.where(first_half, pltpu.roll(y, LANES - ROPE_AXIS_DIM // 2, 1),
                                pltpu.roll(y, ROPE_AXIS_DIM // 2, 1))
            return jnp.where(latent, y * cos + partner * sin, y)

        q_tiles = [rope(head_norm(qkv[:, j * LANES:(j + 1) * LANES], qg)) for j in range(D_MODEL // LANES)]
        for t in range(D_MODEL // LANES):
            grp = t // 2
            src = (4 * (2 * (t % 2)) + grp) // 2
            lo_part = q_tiles[src] if grp % 2 == 0 else pltpu.roll(q_tiles[src], HEAD_DIM, 1)
            hi_part = q_tiles[src + 2] if grp % 2 == 1 else pltpu.roll(q_tiles[src + 2], HEAD_DIM, 1)
            q_ref[rows, t * LANES:(t + 1) * LANES] = jnp.where(low_head, lo_part, hi_part).astype(BF16)
        for j in range(KV_DIM // LANES):
            y = head_norm(qkv[:, D_MODEL + j * LANES:D_MODEL + (j + 1) * LANES], kg)
            k_ref[rows, j * LANES:(j + 1) * LANES] = rope(y).astype(BF16)
            kc_ref[rows, j * LANES:(j + 1) * LANES] = y
        v = qkv[:, D_MODEL + KV_DIM:]
        v_ref[rows, :] = v.astype(BF16)
        vc_ref[rows, :] = v


def _qkv(xs, mod, g, w_qkv, j, qg2, kg2, cos_t, sin_t, bd):
    tile = lambda i: (i, 0)
    const = lambda i: (0, 0)
    return pl.pallas_call(
        functools.partial(_qkv_kernel, n_x=len(xs)),
        grid=(N_SUPER,),
        in_specs=_x_specs(len(xs), SUPER, N_PROMPT_SUPER) + [
            _mod_spec(0, lambda i: i),
            pl.BlockSpec((1, D_MODEL), const),
            pl.BlockSpec((1, D_MODEL, QKV_DIM), lambda i: (j, 0, 0)),
            pl.BlockSpec((1, LANES), const),
            pl.BlockSpec((1, LANES), const),
            pl.BlockSpec((DEC_SEQ, LANES), const),
            pl.BlockSpec((DEC_SEQ, LANES), const),
            pl.BlockSpec((LANES, LANES), const),
        ],
        out_specs=[
            pl.BlockSpec((SUPER, D_MODEL), tile),
            pl.BlockSpec((SUPER, KV_DIM), tile),
            pl.BlockSpec((SUPER, KV_DIM), tile),
            pl.BlockSpec((SUPER, KV_DIM), tile),
            pl.BlockSpec((SUPER, KV_DIM), tile),
        ],
        out_shape=[
            jax.ShapeDtypeStruct((N_TOK, D_MODEL), BF16),
            jax.ShapeDtypeStruct((N_TOK, KV_DIM), BF16),
            jax.ShapeDtypeStruct((N_TOK, KV_DIM), BF16),
            jax.ShapeDtypeStruct((N_TOK, KV_DIM), F32),
            jax.ShapeDtypeStruct((N_TOK, KV_DIM), F32),
        ],
        scratch_shapes=[pltpu.VMEM((D_MODEL, QKV_DIM), BF16)],
        compiler_params=_params(("arbitrary",)),
        name="qkv_proj",
    )(*xs, mod, g, w_qkv, qg2, kg2, cos_t, sin_t, bd)


def _attn_heads(q_ref, rows, n_rows, k_all, v_all, o_ref):
    lane = lax.broadcasted_iota(I32, (n_rows, KV_DIM), 1)
    for g in range(N_GROUPS):
        qs = q_ref[rows, g * KV_DIM:(g + 1) * KV_DIM]
        acc = jnp.zeros((n_rows, KV_DIM), F32)
        for kv in range(N_KV_HEADS):
            mask = (lane >= kv * HEAD_DIM) & (lane < (kv + 1) * HEAD_DIM)
            s = _dot_nt(jnp.where(mask, qs, jnp.zeros_like(qs)), k_all)
            p = jnp.exp(s - jnp.max(s, axis=-1, keepdims=True))
            inv = 1.0 / jnp.sum(p, axis=-1, keepdims=True)
            o = _dot(p.astype(BF16), v_all)
            acc = acc + jnp.where(mask, o * inv, 0.0)
        o_ref[rows, g * KV_DIM:(g + 1) * KV_DIM] = acc.astype(BF16)


def _attn_prompt_kernel(q_ref, k_ref, v_ref, o_ref):
    for r in range(PROMPT_ATTN_SEQS):
        rows = pl.ds(r * SEQ, SEQ)
        _attn_heads(q_ref, rows, SEQ, k_ref[rows, :], v_ref[rows, :], o_ref)


def _attn_sample_kernel(q_ref, k_ref, v_ref, ck_ref, cv_ref, o_ref, kall_ref, vall_ref):
    @pl.when(pl.program_id(1) == 0)
    def _():
        kall_ref[:PAST_LEN] = ck_ref[0, 0].astype(BF16)
        kall_ref[PAST_LEN:] = k_ref[...]
        vall_ref[:PAST_LEN] = cv_ref[0, 0].astype(BF16)
        vall_ref[PAST_LEN:] = v_ref[...]

    _attn_heads(q_ref, pl.ds(0, ATTN_TQ), ATTN_TQ, kall_ref[...], vall_ref[...], o_ref)


def _attention(q, k, v, cache_k4, cache_v4, j):
    p_rows = PROMPT_ATTN_SEQS * SEQ
    o_p = pl.pallas_call(
        _attn_prompt_kernel,
        grid=(BATCH // PROMPT_ATTN_SEQS,),
        in_specs=[
            pl.BlockSpec((p_rows, D_MODEL), lambda b: (b, 0)),
            pl.BlockSpec((p_rows, KV_DIM), lambda b: (b, 0)),
            pl.BlockSpec((p_rows, KV_DIM), lambda b: (b, 0)),
        ],
        out_specs=pl.BlockSpec((p_rows, D_MODEL), lambda b: (b, 0)),
        out_shape=jax.ShapeDtypeStruct((N_PROMPT_TOK, D_MODEL), BF16),
        compiler_params=_params(("arbitrary",)),
        name="attn_prompt",
    )(q, k, v)
    q_blocks = DEC_SEQ // ATTN_TQ
    first_q = N_PROMPT_TOK // ATTN_TQ
    first_k = N_PROMPT_TOK // DEC_SEQ
    o_s = pl.pallas_call(
        _attn_sample_kernel,
        grid=(DEC_BATCH, q_blocks),
        in_specs=[
            pl.BlockSpec((ATTN_TQ, D_MODEL), lambda b, t: (first_q + b * q_blocks + t, 0)),
            pl.BlockSpec((DEC_SEQ, KV_DIM), lambda b, t: (first_k + b, 0)),
            pl.BlockSpec((DEC_SEQ, KV_DIM), lambda b, t: (first_k + b, 0)),
            pl.BlockSpec((1, 1, PAST_LEN, KV_DIM), lambda b, t: (b, j, 0, 0)),
            pl.BlockSpec((1, 1, PAST_LEN, KV_DIM), lambda b, t: (b, j, 0, 0)),
        ],
        out_specs=pl.BlockSpec((ATTN_TQ, D_MODEL), lambda b, t: (b * q_blocks + t, 0)),
        out_shape=jax.ShapeDtypeStruct((DEC_BATCH * DEC_SEQ, D_MODEL), BF16),
        scratch_shapes=[pltpu.VMEM((PAST_LEN + DEC_SEQ, KV_DIM), BF16),
                        pltpu.VMEM((PAST_LEN + DEC_SEQ, KV_DIM), BF16)],
        compiler_params=_params(("arbitrary", "arbitrary")),
        name="attn_sample",
    )(q, k, v, cache_k4, cache_v4)
    return o_p, o_s


def _oproj_kernel(*refs, n_x):
    x_refs = refs[:n_x]
    op_ref, os_ref = refs[n_x:n_x + 2]
    w_refs = refs[n_x + 2:n_x + 2 + N_GROUPS]
    m_ref, out_ref, wbf_ref = refs[n_x + 2 + N_GROUPS:]
    i = pl.program_id(0)

    @pl.when(i == 0)
    def _():
        for g in range(N_GROUPS):
            wbf_ref[g * KV_DIM:(g + 1) * KV_DIM, :] = w_refs[g][0, :, 0].reshape(KV_DIM, D_MODEL).astype(BF16)

    prompt = i < N_PROMPT_TILES
    gate = m_ref[0, 2, 0]
    for sb in range(TILE_M // PROJ_SUB):
        rows = pl.ds(sb * PROJ_SUB, PROJ_SUB)
        o = jnp.where(prompt, op_ref[rows, :], os_ref[rows, :])
        x = x_refs[0][rows, :] if n_x == 1 else jnp.where(prompt, x_refs[0][rows, :], x_refs[1][rows, :])
        out_ref[rows, :] = x + gate * _dot(o, wbf_ref[...])


def _oproj(xs, o_p, o_s, w_o5, j, mod):
    w_specs = [pl.BlockSpec((1, N_KV_HEADS, 1, HEAD_DIM, D_MODEL), lambda i, g=g: (j, 0, g, 0, 0))
               for g in range(N_GROUPS)]
    return pl.pallas_call(
        functools.partial(_oproj_kernel, n_x=len(xs)),
        grid=(N_TILES,),
        in_specs=_x_specs(len(xs), TILE_M, N_PROMPT_TILES) + [
            pl.BlockSpec((TILE_M, D_MODEL), lambda i: (jnp.minimum(i, N_PROMPT_TILES - 1), 0)),
            pl.BlockSpec((TILE_M, D_MODEL), lambda i: (jnp.maximum(i - N_PROMPT_TILES, 0), 0)),
        ] + w_specs + [_mod_spec(0, _tile_super)],
        out_specs=pl.BlockSpec((TILE_M, D_MODEL), lambda i: (i, 0)),
        out_shape=jax.ShapeDtypeStruct((N_TOK, D_MODEL), F32),
        scratch_shapes=[pltpu.VMEM((D_MODEL, D_MODEL), BF16)],
        compiler_params=_params(("arbitrary",)),
        name="attn_out_proj",
    )(*xs, o_p, o_s, *([w_o5] * N_GROUPS), mod)


PAD_ROWS = CONV_HALO + SEQ_PER_SUPER * (SEQ + CONV_HALO)


def _conv_kernel(x_ref, g_ref, w1_ref, b1_ref, wdw_ref, bdw_ref, lng_ref, lnb_ref, w2_ref, b2_ref, m_ref,
                 out_ref, pad_ref, cv_ref, w1bf_ref, wbf_ref):
    ss = pl.program_id(0)

    @pl.when(ss == 0)
    def _():
        w1bf_ref[...] = w1_ref[0].astype(BF16)
        wbf_ref[...] = w2_ref[0].astype(BF16)

    seq_gap = jnp.where(ss < N_PROMPT_SUPER, CONV_HALO, 0)
    pad_ref[...] = jnp.zeros_like(pad_ref)
    g_mix = g_ref[...]
    shift = m_ref[0, 0, 0]
    scale = m_ref[0, 1, 0]
    for s in range(SEQ_PER_SUPER):
        h = _norm_mod(x_ref[pl.ds(s * SEQ, SEQ), :], g_mix, shift, scale).astype(BF16)
        z = _dot(h, w1bf_ref[...]) + b1_ref[0]
        u = z[:, :D_MODEL] * _sigmoid(z[:, D_MODEL:])
        for j in range(D_MODEL // LANES):
            pad_ref[j, pl.ds(CONV_HALO + s * SEQ + s * seq_gap, SEQ), :] = u[:, j * LANES:(j + 1) * LANES]

    def block(c, carry):
        r0 = pl.multiple_of(c * CONV_ROWS, CONV_ROWS)
        base = r0 + (c // (SEQ // CONV_ROWS)) * seq_gap + (CONV_HALO - CONV_K // 2)
        for j in range(D_MODEL // LANES):
            cols = slice(j * LANES, (j + 1) * LANES)
            acc = jnp.zeros((CONV_ROWS, LANES), F32)
            for kk in range(CONV_K):
                acc = acc + wdw_ref[0, pl.ds(kk, 1), cols] * pad_ref[j, pl.ds(base + kk, CONV_ROWS), :]
            cv_ref[pl.ds(r0, CONV_ROWS), cols] = acc + bdw_ref[0, :, cols]
        return carry

    lax.fori_loop(0, SUPER // CONV_ROWS, block, 0)

    gate = m_ref[0, 2, 0]
    for sb in range(SUPER // CONV_TAIL_ROWS):
        rows = pl.ds(sb * CONV_TAIL_ROWS, CONV_TAIL_ROWS)
        cv = cv_ref[rows, :]
        mu = jnp.mean(cv, axis=-1, keepdims=True)
        cen = cv - mu
        var = jnp.mean(cen * cen, axis=-1, keepdims=True)
        y = cen * lax.rsqrt(var + EPS) * lng_ref[0] + lnb_ref[0]
        act = (y * _sigmoid(y)).astype(BF16)
        out_ref[rows, :] = x_ref[rows, :] + gate * (_dot(act, wbf_ref[...]) + b2_ref[0])


def _conv(x, g, w1, b1, wdw, bdw, lng, lnb, w2, b2, mod, j):
    vec = pl.BlockSpec((1, 1, D_MODEL), lambda s: (j, 0, 0))
    return pl.pallas_call(
        _conv_kernel,
        grid=(N_SUPER,),
        in_specs=[
            pl.BlockSpec((SUPER, D_MODEL), lambda s: (s, 0)),
            pl.BlockSpec((1, D_MODEL), lambda s: (0, 0)),
            pl.BlockSpec((1, D_MODEL, 2 * D_MODEL), lambda s: (j, 0, 0)),
            pl.BlockSpec((1, 1, 2 * D_MODEL), lambda s: (j, 0, 0)),
            pl.BlockSpec((1, CONV_K, D_MODEL), lambda s: (j, 0, 0)),
            vec, vec, vec,
            pl.BlockSpec((1, D_MODEL, D_MODEL), lambda s: (j, 0, 0)),
            vec,
            _mod_spec(0, lambda s: s),
        ],
        out_specs=pl.BlockSpec((SUPER, D_MODEL), lambda s: (s, 0)),
        out_shape=jax.ShapeDtypeStruct((N_TOK, D_MODEL), F32),
        scratch_shapes=[pltpu.VMEM((D_MODEL // LANES, PAD_ROWS, LANES), F32),
                        pltpu.VMEM((SUPER, D_MODEL), F32),
                        pltpu.VMEM((D_MODEL, 2 * D_MODEL), BF16),
                        pltpu.VMEM((D_MODEL, D_MODEL), BF16)],
        compiler_params=_params(("arbitrary",)),
        name="conv_module",
    )(x, g, w1, b1, wdw, bdw, lng, lnb, w2, b2, mod)


def _route(aff, cap, tri):
    rows = aff.shape[0]
    capf = float(cap)

    def count_ge(bits):
        return jnp.sum((aff >= lax.bitcast_convert_type(bits, F32)).astype(F32), axis=1, keepdims=True)

    thr = jnp.zeros((rows, 1), I32)
    top = thr | (1 << 30)
    thr = jnp.where(count_ge(top) >= capf, top, thr)
    for shift in range(27, -1, -3):
        digit = jnp.zeros((rows, 1), I32)
        for d in range(1, 8):
            digit = digit + (count_ge(thr | (d << shift)) >= capf).astype(I32)
        thr = thr | (digit << shift)
    thr_f = lax.bitcast_convert_type(thr, F32)
    above = aff > thr_f
    tied = aff == thr_f
    need = capf - jnp.sum(above.astype(F32), axis=1, keepdims=True)
    tied_rank = _dot(tied.astype(BF16), tri)
    sel = above | (tied & (tied_rank <= need))
    return jnp.where(sel, _dot(sel.astype(BF16), tri) - 1.0, -1.0)


def _pad_rows_t(a, fill):
    pad = jnp.full((LANES - N_EXPERTS, a.shape[1]), fill, F32)
    return jnp.concatenate([a, pad], axis=0).T


def _moe_select_kernel(*refs, n_x):
    if n_x:
        x_refs = refs[:n_x]
        op_ref, os_ref = refs[n_x:n_x + 2]
        w_refs = refs[n_x + 2:n_x + 2 + N_GROUPS]
        mm_ref = refs[n_x + 2 + N_GROUPS]
        (m_ref, g_ref, wrt_ref, xs_ref, slot_ref, gate_ref, x1_ref,
         h_ref, tri_ref, p_ref, wbf_ref) = refs[n_x + 3 + N_GROUPS:]
    else:
        x_ref, m_ref, g_ref, wrt_ref, xs_ref, slot_ref, gate_ref, h_ref, tri_ref, p_ref = refs
    ss = pl.program_id(0)

    @pl.when(ss == 0)
    def _():
        r = lax.broadcasted_iota(I32, (SUPER, SUPER), 0)
        c = lax.broadcasted_iota(I32, (SUPER, SUPER), 1)
        tri_ref[...] = (r <= c).astype(BF16)
        if n_x:
            for g in range(N_GROUPS):
                wbf_ref[g * KV_DIM:(g + 1) * KV_DIM, :] = w_refs[g][0, :, 0].reshape(KV_DIM, D_MODEL).astype(BF16)

    g_ffn = g_ref[...]
    shift_f = m_ref[0, 0, 0]
    scale_f = m_ref[0, 1, 0]
    if n_x:
        prompt = ss < N_PROMPT_SUPER
        gate_a = mm_ref[0, 2, 0]
        for sb in range(SUPER // PROJ_SUB):
            rows = pl.ds(sb * PROJ_SUB, PROJ_SUB)
            o = jnp.where(prompt, op_ref[rows, :], os_ref[rows, :])
            x = x_refs[0][rows, :] if n_x == 1 else jnp.where(prompt, x_refs[0][rows, :], x_refs[1][rows, :])
            x1 = x + gate_a * _dot(o, wbf_ref[...])
            x1_ref[rows, :] = x1
            h_ref[rows, :] = _norm_mod(x1, g_ffn, shift_f, scale_f).astype(BF16)
    else:
        h_ref[...] = _norm_mod(x_ref[...], g_ffn, shift_f, scale_f).astype(BF16)

    logit = _dot_nt(wrt_ref[0].astype(BF16), h_ref[...])
    ex = jnp.exp(logit - jnp.max(logit, axis=0, keepdims=True))
    aff = ex / jnp.sum(ex, axis=0, keepdims=True)

    @pl.when(ss < N_PROMPT_SUPER)
    def _():
        aff_r = jnp.concatenate([aff[:, b * SEQ:(b + 1) * SEQ] for b in range(SEQ_PER_SUPER)], axis=0)
        pos = _route(aff_r, SEQ_SLOTS, tri_ref[:SEQ, :SEQ])
        sub = lax.broadcasted_iota(I32, (SEQ_SLOTS, SEQ), 0).astype(F32)
        for b in range(SEQ_PER_SUPER):
            for e in range(N_EXPERTS):
                r = b * N_EXPERTS + e
                p_ref[e * SEQ_SLOTS:(e + 1) * SEQ_SLOTS, :] = (sub == pos[r:r + 1, :]).astype(BF16)
            rows = _dot(p_ref[...], h_ref[b * SEQ:(b + 1) * SEQ, :])
            xs_ref[:, b * SEQ_SLOTS:(b + 1) * SEQ_SLOTS, :] = (
                rows.astype(BF16).reshape(N_EXPERTS, SEQ_SLOTS, D_MODEL))
            slot_ref[b * SEQ:(b + 1) * SEQ, :] = _pad_rows_t(pos[b * N_EXPERTS:(b + 1) * N_EXPERTS], -1.0)
            gate_ref[b * SEQ:(b + 1) * SEQ, :] = _pad_rows_t(aff[:, b * SEQ:(b + 1) * SEQ], 0.0)

    @pl.when(ss >= N_PROMPT_SUPER)
    def _():
        pos = _route(aff, SLOTS, tri_ref[...])
        sub = lax.broadcasted_iota(I32, (SLOTS, SUPER), 0).astype(F32)
        for e0 in range(0, N_EXPERTS, GATHER_EXPERTS):
            onehot = jnp.concatenate([(sub == pos[e:e + 1, :]).astype(BF16)
                                      for e in range(e0, e0 + GATHER_EXPERTS)], axis=0)
            rows = _dot(onehot, h_ref[...])
            xs_ref[e0:e0 + GATHER_EXPERTS] = rows.astype(BF16).reshape(GATHER_EXPERTS, SLOTS, D_MODEL)
        slot_ref[...] = _pad_rows_t(pos, -1.0)
        gate_ref[...] = _pad_rows_t(aff, 0.0)


def _moe_select(x, mod, layer, g, wr_t, attn=None):
    in_specs = [
        _mod_spec(1, lambda s: s),
        pl.BlockSpec((1, D_MODEL), lambda s: (0, 0)),
        pl.BlockSpec((1, N_EXPERTS, D_MODEL), lambda s: (layer, 0, 0)),
    ]
    out_specs = [
        pl.BlockSpec((N_EXPERTS, SLOTS, D_MODEL), lambda s: (0, s, 0)),
        pl.BlockSpec((SUPER, LANES), lambda s: (s, 0)),
        pl.BlockSpec((SUPER, LANES), lambda s: (s, 0)),
    ]
    out_shape = [
        jax.ShapeDtypeStruct((N_EXPERTS, N_SUPER * SLOTS, D_MODEL), BF16),
        jax.ShapeDtypeStruct((N_TOK, LANES), F32),
        jax.ShapeDtypeStruct((N_TOK, LANES), F32),
    ]
    scratch = [pltpu.VMEM((SUPER, D_MODEL), BF16),
               pltpu.VMEM((SUPER, SUPER), BF16),
               pltpu.VMEM((N_EXPERTS * SEQ_SLOTS, SEQ), BF16)]
    if attn is None:
        n_x = 0
        in_specs = [pl.BlockSpec((SUPER, D_MODEL), lambda s: (s, 0))] + in_specs
        args = [x, mod, g, wr_t]
    else:
        xs, o_p, o_s, w_o5, j = attn
        n_x = len(xs)
        w_specs = [pl.BlockSpec((1, N_KV_HEADS, 1, HEAD_DIM, D_MODEL), lambda s, grp=grp: (j, 0, grp, 0, 0))
                   for grp in range(N_GROUPS)]
        in_specs = _x_specs(n_x, SUPER, N_PROMPT_SUPER) + [
            pl.BlockSpec((SUPER, D_MODEL), lambda s: (jnp.minimum(s, N_PROMPT_SUPER - 1), 0)),
            pl.BlockSpec((SUPER, D_MODEL), lambda s: (jnp.maximum(s - N_PROMPT_SUPER, 0), 0)),
        ] + w_specs + [_mod_spec(0, lambda s: s)] + in_specs
        out_specs.append(pl.BlockSpec((SUPER, D_MODEL), lambda s: (s, 0)))
        out_shape.append(jax.ShapeDtypeStruct((N_TOK, D_MODEL), F32))
        scratch.append(pltpu.VMEM((D_MODEL, D_MODEL), BF16))
        args = [*xs, o_p, o_s, *([w_o5] * N_GROUPS), mod, mod, g, wr_t]
    return pl.pallas_call(
        functools.partial(_moe_select_kernel, n_x=n_x),
        grid=(N_SUPER,),
        in_specs=in_specs,
        out_specs=out_specs,
        out_shape=out_shape,
        scratch_shapes=scratch,
        compiler_params=_params(("arbitrary",)),
        name="moe_select",
    )(*args)


def _moe_ffn_kernel(x_ref, wg_ref, wu_ref, wd_ref, *refs, with_mod):
    if with_mod:
        cond_ref, wm_ref, bm_ref, y_ref, mod_ref, acc_ref = refs
        c = cond_ref[...]
        s = (c * _sigmoid(c)).astype(BF16)
        mod_ref[0] = _dot(s, wm_ref[0].astype(BF16)) + bm_ref[0, 0]
    else:
        y_ref, acc_ref = refs
    f = pl.program_id(1)
    x = x_ref[0]
    a = _dot(x, wg_ref[0, 0].astype(BF16))
    u = _dot(x, wu_ref[0, 0].astype(BF16))
    hid = (a * _sigmoid(a) * u).astype(BF16)
    acc_ref[f] = _dot(hid, wd_ref[0, 0].astype(BF16))

    @pl.when(f == N_FF_CHUNKS - 1)
    def _():
        y_ref[0] = (acc_ref[0] + acc_ref[1]).astype(BF16)


def _moe_ffn(xs, w_gate, w_up, w_down, layer, cond8=None, w_mod=None, b_mod=None):
    assert N_FF_CHUNKS == 2
    rows = N_SUPER * SLOTS
    with_mod = cond8 is not None
    in_specs = [
        pl.BlockSpec((1, rows, D_MODEL), lambda e, f: (e, 0, 0)),
        pl.BlockSpec((1, 1, D_MODEL, FF_CHUNK), lambda e, f: (layer, e, 0, f)),
        pl.BlockSpec((1, 1, D_MODEL, FF_CHUNK), lambda e, f: (layer, e, 0, f)),
        pl.BlockSpec((1, 1, FF_CHUNK, D_MODEL), lambda e, f: (layer, e, f, 0)),
    ]
    out_specs = [pl.BlockSpec((1, rows, D_MODEL), lambda e, f: (e, 0, 0))]
    out_shape = [jax.ShapeDtypeStruct((N_EXPERTS, rows, D_MODEL), BF16)]
    args = [xs, w_gate, w_up, w_down]
    if with_mod:
        n_chunks = N_MOD * D_MODEL // MOD_CHUNK
        assert n_chunks <= N_EXPERTS * N_FF_CHUNKS
        chunk = lambda e, f: jnp.minimum(e * N_FF_CHUNKS + f, n_chunks - 1)
        in_specs += [
            pl.BlockSpec((N_SUPER, D_MODEL), lambda e, f: (0, 0)),
            pl.BlockSpec((1, D_MODEL, MOD_CHUNK), lambda e, f: (layer + 1, 0, chunk(e, f))),
            pl.BlockSpec((1, 1, 1, MOD_CHUNK), lambda e, f: (layer + 1, chunk(e, f), 0, 0)),
        ]
        out_specs.append(pl.BlockSpec((1, N_SUPER, MOD_CHUNK), lambda e, f: (chunk(e, f), 0, 0)))
        out_shape.append(jax.ShapeDtypeStruct((n_chunks, N_SUPER, MOD_CHUNK), F32))
        args += [cond8, w_mod, b_mod.reshape(DEPTH, n_chunks, 1, MOD_CHUNK)]
    outs = pl.pallas_call(
        functools.partial(_moe_ffn_kernel, with_mod=with_mod),
        grid=(N_EXPERTS, N_FF_CHUNKS),
        in_specs=in_specs,
        out_specs=out_specs,
        out_shape=out_shape,
        scratch_shapes=[pltpu.VMEM((N_FF_CHUNKS, rows, D_MODEL), F32)],
        compiler_params=_params(("arbitrary", "arbitrary")),
        name="moe_ffn",
    )(*args)
    if not with_mod:
        return outs[0], None
    mod = outs[1].transpose(1, 0, 2).reshape(N_SUPER, N_MOD, D_MODEL).transpose(1, 0, 2)
    return outs[0], mod.reshape(2, 3, N_SUPER, 1, D_MODEL)


def _moe_combine_kernel(x_ref, y_ref, slot_ref, gate_ref, spread_ref, m_ref, *refs, final):
    out_refs, s_ref = refs[:-1], refs[-1]
    ss = pl.program_id(0)
    gate_f = m_ref[0, 2, 0]

    @pl.when(ss < N_PROMPT_SUPER)
    def _():
        lane_slot = (lax.broadcasted_iota(I32, (SEQ, LANES), 1) % SEQ_SLOTS).astype(F32)
        for b in range(SEQ_PER_SUPER):
            rows = pl.ds(b * SEQ, SEQ)
            slot_b = slot_ref[rows, :].astype(BF16)
            gate_b = gate_ref[rows, :].astype(BF16)
            for t in range(N_EXPERTS * SEQ_SLOTS // LANES):
                hit = _dot(slot_b, spread_ref[t]) == lane_slot
                s_ref[rows, t * LANES:(t + 1) * LANES] = jnp.where(hit, _dot(gate_b, spread_ref[t]), 0.0).astype(BF16)
            y_b = y_ref[:, b * SEQ_SLOTS:(b + 1) * SEQ_SLOTS, :].reshape(N_EXPERTS * SEQ_SLOTS, D_MODEL)
            res = x_ref[rows, :] + gate_f * _dot(s_ref[rows, :N_EXPERTS * SEQ_SLOTS], y_b)
            out_refs[0][rows, :] = res

    @pl.when(ss >= N_PROMPT_SUPER)
    def _():
        lane = lax.broadcasted_iota(I32, (SUPER, SLOTS), 1).astype(F32)
        moe = None
        for e0 in range(0, N_EXPERTS, SCATTER_EXPERTS):
            s_cols = jnp.concatenate(
                [jnp.where(slot_ref[:, e:e + 1] == lane, gate_ref[:, e:e + 1], 0.0).astype(BF16)
                 for e in range(e0, e0 + SCATTER_EXPERTS)], axis=1)
            part = _dot(s_cols, y_ref[e0:e0 + SCATTER_EXPERTS].reshape(SCATTER_EXPERTS * SLOTS, D_MODEL))
            moe = part if moe is None else moe + part
        out_refs[-1][...] = x_ref[...] + gate_f * moe


def _moe_combine(x, ys, slot_t, gate_t, spread, mod, final):
    if final:
        out_specs = [pl.BlockSpec((SUPER, D_MODEL), lambda s: (jnp.minimum(s, N_PROMPT_SUPER - 1), 0)),
                     pl.BlockSpec((SUPER, D_MODEL), lambda s: (jnp.maximum(s - N_PROMPT_SUPER, 0), 0))]
        out_shape = [jax.ShapeDtypeStruct((N_PROMPT_TOK, D_MODEL), F32),
                     jax.ShapeDtypeStruct((N_TOK - N_PROMPT_TOK, D_MODEL), F32)]
    else:
        out_specs = [pl.BlockSpec((SUPER, D_MODEL), lambda s: (s, 0))]
        out_shape = [jax.ShapeDtypeStruct((N_TOK, D_MODEL), F32)]
    return pl.pallas_call(
        functools.partial(_moe_combine_kernel, final=final),
        grid=(N_SUPER,),
        in_specs=[
            pl.BlockSpec((SUPER, D_MODEL), lambda s: (s, 0)),
            pl.BlockSpec((N_EXPERTS, SLOTS, D_MODEL), lambda s: (0, s, 0)),
            pl.BlockSpec((SUPER, LANES), lambda s: (s, 0)),
            pl.BlockSpec((SUPER, LANES), lambda s: (s, 0)),
            pl.BlockSpec((N_EXPERTS * SEQ_SLOTS // LANES, LANES, LANES), lambda s: (0, 0, 0)),
            _mod_spec(1, lambda s: s),
        ],
        out_specs=out_specs,
        out_shape=out_shape,
        scratch_shapes=[pltpu.VMEM((SUPER, N_EXPERTS * SEQ_SLOTS), BF16)],
        compiler_params=_params(("arbitrary",)),
        name="moe_combine",
    )(x, ys, slot_t, gate_t, spread, mod)


def _rope_tables():
    t = np.arange(DEC_SEQ)
    row = (t // GRID_W).astype(np.float32)
    col = (t % GRID_W).astype(np.float32)
    inv = (1.0 / (ROPE_THETA ** (np.arange(0, ROPE_AXIS_DIM, 2, dtype=np.float32) / ROPE_AXIS_DIM))).astype(np.float32)
    d = np.arange(LANES) % HEAD_DIM
    use_col = (d // ROPE_AXIS_DIM) == 1
    second_half = ((d % ROPE_AXIS_DIM) // (ROPE_AXIS_DIM // 2)) == 1
    ang = (np.where(use_col[None, :], col[:, None], row[:, None]) * inv[d % (ROPE_AXIS_DIM // 2)][None, :]).astype(np.float32)
    cos = np.cos(ang).astype(np.float32)
    sin = (np.where(second_half[None, :], 1.0, -1.0) * np.sin(ang)).astype(np.float32)
    return jnp.asarray(cos), jnp.asarray(sin)


def _head_block_ones():
    lane = np.arange(LANES)
    return jnp.asarray((lane[:, None] // HEAD_DIM) == (lane[None, :] // HEAD_DIM), dtype=BF16)


def _spread_tables():
    tiles = N_EXPERTS * SEQ_SLOTS // LANES
    e = np.arange(LANES)[None, :, None]
    lane = np.arange(LANES)[None, None, :]
    tile = np.arange(tiles)[:, None, None]
    return jnp.asarray(e == tile * (LANES // SEQ_SLOTS) + lane // SEQ_SLOTS, dtype=BF16)


def kernel(x_prompt, x_sample, cache_k, cache_v, c, c_ctx, w_mod, b_mod, norm_mix, norm_ffn, attn_w_qkv, attn_q_norm, attn_k_norm, attn_w_o, conv_w_pw1, conv_b_pw1, conv_w_dw, conv_b_dw, conv_ln_g, conv_ln_b, conv_w_pw2, conv_b_pw2, moe_w_router, moe_w_gate, moe_w_up, moe_w_down):
    xs = [x_prompt.reshape(N_PROMPT_TOK, D_MODEL), x_sample.reshape(N_TOK - N_PROMPT_TOK, D_MODEL)]
    cond8 = jnp.concatenate([jnp.broadcast_to(c_ctx[None, :], (N_PROMPT_SUPER, D_MODEL)), c], axis=0)
    mod = _adaln_first(cond8, w_mod, b_mod)

    cos_t, sin_t = _rope_tables()
    bd = _head_block_ones()
    spread = _spread_tables()
    wr_t = jnp.swapaxes(moe_w_router, 1, 2)
    cache_k4 = cache_k.reshape(DEC_BATCH, -1, PAST_LEN, KV_DIM)
    cache_v4 = cache_v.reshape(DEC_BATCH, -1, PAST_LEN, KV_DIM)
    w_o5 = attn_w_o.reshape(-1, N_KV_HEADS, N_GROUPS, HEAD_DIM, D_MODEL)

    new_k, new_v = [], []
    for i in range(DEPTH):
        j = i // N_MIXERS
        g_mix = norm_mix[i][None, :]
        g_ffn = norm_ffn[i][None, :]
        if i % N_MIXERS == 0:
            qg2 = jnp.tile(attn_q_norm[j], LANES // HEAD_DIM)[None, :]
            kg2 = jnp.tile(attn_k_norm[j], LANES // HEAD_DIM)[None, :]
            q, k, v, kc, vc = _qkv(xs, mod, g_mix, attn_w_qkv, j, qg2, kg2, cos_t, sin_t, bd)
            new_k.append(kc[:N_PROMPT_TOK].reshape(BATCH, SEQ, N_KV_HEADS, HEAD_DIM))
            new_v.append(vc[:N_PROMPT_TOK].reshape(BATCH, SEQ, N_KV_HEADS, HEAD_DIM))
            o_p, o_s = _attention(q, k, v, cache_k4, cache_v4, j)
            if len(xs) == 1:
                xg, slot_t, gate_t, x = _moe_select(None, mod, i, g_ffn, wr_t, attn=(xs, o_p, o_s, w_o5, j))
            else:
                x = _oproj(xs, o_p, o_s, w_o5, j, mod)
                xg, slot_t, gate_t = _moe_select(x, mod, i, g_ffn, wr_t)
        else:
            x = _conv(xs[0], g_mix, conv_w_pw1, conv_b_pw1[:, None, :], conv_w_dw, conv_b_dw[:, None, :],
                      conv_ln_g[:, None, :], conv_ln_b[:, None, :], conv_w_pw2, conv_b_pw2[:, None, :], mod, j)
            xg, slot_t, gate_t = _moe_select(x, mod, i, g_ffn, wr_t)
        last = i == DEPTH - 1
        ys, next_mod = _moe_ffn(xg, moe_w_gate, moe_w_up, moe_w_down, i,
                                *(() if last else (cond8, w_mod, b_mod)))
        xs = _moe_combine(x, ys, slot_t, gate_t, spread, mod, final=last)
        mod = next_mod

    y_prompt = xs[0].reshape(BATCH, SEQ, D_MODEL)
    y_sample = xs[1].reshape(DEC_BATCH, DEC_SEQ, D_MODEL)
    return (y_prompt, y_sample, jnp.stack(new_k, axis=1), jnp.stack(new_v, axis=1))
```
